```python
import math
import jax, jax.numpy as jnp
from jax import lax
import numpy as np

D_MODEL = 1024
BATCH = 4
SEQ = 8192
DEPTH = 4

CHUNK = 64
Q_BLOCK = 128
N_MIXERS = 4
D_FF = 2816
RMS_EPS = 1e-6
LN_EPS = 1e-5
MASK_VALUE = -1e30

CONV_WIDTH = 31

DIFF_HEADS = 8
DIFF_QK_DIM = 64
DIFF_V_DIM = 2 * DIFF_QK_DIM
LAMBDA_INIT_BASE = 0.8
LAMBDA_INIT_SCALE = 0.6
LAMBDA_INIT_DECAY = 0.3

REL_BUCKETS = 32
REL_MAX_DIST = 128

SB_HEADS = 16
SB_HEAD_DIM = 64

MLA_HEADS = 16
MLA_Q_LORA = 384
MLA_KV_LORA = 256
MLA_NOPE = 64
MLA_ROPE = 32
MLA_V = 64
ROPE_BASE = 10000.0

N_CONV_LAYERS = (DEPTH + 3) // 4
N_DIFF_LAYERS = (DEPTH + 2) // 4
N_SB_LAYERS = (DEPTH + 1) // 4
N_MLA_LAYERS = DEPTH // 4

kernel_name = "hybrid_chunk_causal_macaron_trunk"


def rms_norm(x, g):
    xf = x.astype(jnp.float32)
    y = xf * lax.rsqrt(jnp.mean(xf * xf, axis=-1, keepdims=True) + RMS_EPS)
    return (y * g.astype(jnp.float32)).astype(x.dtype)


def layer_norm(x, g, b):
    xf = x.astype(jnp.float32)
    xc = xf - jnp.mean(xf, axis=-1, keepdims=True)
    y = xc * lax.rsqrt(jnp.mean(xc * xc, axis=-1, keepdims=True) + LN_EPS)
    return (y * g.astype(jnp.float32) + b.astype(jnp.float32)).astype(x.dtype)


def swiglu_ffn(x, g, w_in, w_out):
    a, u = jnp.split(rms_norm(x, g) @ w_in, 2, axis=-1)
    return (jax.nn.silu(a) * u) @ w_out


def chunk_causal_mask(q_pos, k_pos):
    return (k_pos[None, :] // CHUNK) <= (q_pos[:, None] // CHUNK)


def t5_bucket(rel):
    nb = REL_BUCKETS // 2
    bucket = jnp.where(rel > 0, nb, 0)
    n = jnp.abs(rel)
    max_exact = nb // 2
    n_f = jnp.maximum(n, 1).astype(jnp.float32)
    large = max_exact + (jnp.log(n_f / max_exact) / math.log(REL_MAX_DIST / max_exact)
                         * (nb - max_exact)).astype(jnp.int32)
    large = jnp.minimum(large, nb - 1)
    return bucket + jnp.where(n < max_exact, n, large)


def rope_tables(seq, dtype):
    inv = ROPE_BASE ** (-jnp.arange(0, MLA_ROPE, 2, dtype=jnp.float32) / MLA_ROPE)
    ang = jnp.arange(seq, dtype=jnp.float32)[:, None] * inv[None, :]
    return jnp.cos(ang).astype(dtype), jnp.sin(ang).astype(dtype)


def apply_rope(x, cos, sin):
    x1, x2 = jnp.split(x, 2, axis=-1)
    return jnp.concatenate([x1 * cos - x2 * sin, x1 * sin + x2 * cos], axis=-1)


def conformer_conv_module(h, w_in, b_in, w_dw, b_dw, ln_g, ln_b, w_out, b_out):
    a, gate = jnp.split(h @ w_in + b_in, 2, axis=-1)
    u = a * jax.nn.sigmoid(gate)
    u = lax.conv_general_dilated(
        u, w_dw[:, None, :].astype(u.dtype), window_strides=(1,),
        padding=[(CONV_WIDTH - 1, 0)],
        dimension_numbers=('NWC', 'WIO', 'NWC'),
        feature_group_count=D_MODEL) + b_dw
    u = jax.nn.silu(layer_norm(u, ln_g, ln_b))
    return u @ w_out + b_out


def differential_attention(h, layer_idx, w_in, q_norm_g, k_norm_g, lam, sub_norm_g, w_out, rel_table):
    b, s, _ = h.shape
    hh, d = DIFF_HEADS, DIFF_QK_DIM
    proj = (h @ w_in).reshape(b, s, hh, 4 * d + DIFF_V_DIM)
    q, k, v = jnp.split(proj, [2 * d, 4 * d], axis=-1)
    q = rms_norm(q.reshape(b, s, hh, 2, d), q_norm_g).transpose(0, 2, 3, 1, 4)
    k = rms_norm(k.reshape(b, s, hh, 2, d), k_norm_g).transpose(0, 2, 3, 1, 4)
    v = v.transpose(0, 2, 1, 3)
    lam_init = LAMBDA_INIT_BASE - LAMBDA_INIT_SCALE * math.exp(-LAMBDA_INIT_DECAY * layer_idx)
    lam_full = (jnp.exp(jnp.sum(lam[0] * lam[1]).astype(jnp.float32))
                - jnp.exp(jnp.sum(lam[2] * lam[3]).astype(jnp.float32)) + lam_init)
    scale = d ** -0.5
    pos = jnp.arange(s)
    outs = []
    for blk in range(s // Q_BLOCK):
        q0, q1 = blk * Q_BLOCK, (blk + 1) * Q_BLOCK
        qp, kp = pos[q0:q1], pos[:q1]
        bias = rel_table[t5_bucket(kp[None, :] - qp[:, None])].transpose(2, 0, 1)
        logits = jnp.einsum('bhmqd,bhmkd->bhmqk', q[:, :, :, q0:q1], k[:, :, :, :q1]).astype(jnp.float32) * scale
        logits = logits + bias[None, :, None].astype(jnp.float32)
        logits = jnp.where(chunk_causal_mask(qp, kp), logits, MASK_VALUE)
        p = jax.nn.softmax(logits, axis=-1)
        attn = p[:, :, 0] - lam_full * p[:, :, 1]
        outs.append(jnp.einsum('bhqk,bhkv->bhqv', attn.astype(v.dtype), v[:, :, :q1]))
    o = jnp.concatenate(outs, axis=2)
    o = rms_norm(o, sub_norm_g) * (1.0 - lam_init)
    o = o.transpose(0, 2, 1, 3).reshape(b, s, hh * DIFF_V_DIM)
    return o @ w_out


def stick_breaking_attention(h, w_in, w_out):
    b, s, _ = h.shape
    hh, d = SB_HEADS, SB_HEAD_DIM
    q, k, v = jnp.split((h @ w_in).reshape(b, s, hh, 3 * d), 3, axis=-1)
    q, k, v = q.transpose(0, 2, 1, 3), k.transpose(0, 2, 1, 3), v.transpose(0, 2, 1, 3)
    scale = d ** -0.5
    pos = jnp.arange(s)
    outs = []
    for blk in range(s // Q_BLOCK):
        q0, q1 = blk * Q_BLOCK, (blk + 1) * Q_BLOCK
        qp, kp = pos[q0:q1], pos[:q1]
        z = jnp.einsum('bhqd,bhkd->bhqk', q[:, :, q0:q1], k[:, :, :q1]).astype(jnp.float32) * scale
        causal = kp[None, :] < qp[:, None]
        log_beta = jax.nn.log_sigmoid(z)
        log_keep = jnp.where(causal, jax.nn.log_sigmoid(-z), 0.0)
        tail = lax.cumsum(log_keep, axis=3, reverse=True) - log_keep
        weights = jnp.where(causal, jnp.exp(log_beta + tail), 0.0)
        outs.append(jnp.einsum('bhqk,bhkd->bhqd', weights.astype(v.dtype), v[:, :, :q1]))
    o = jnp.concatenate(outs, axis=2).transpose(0, 2, 1, 3).reshape(b, s, hh * d)
    return o @ w_out


def latent_attention(h, w_a, q_a_norm, kv_a_norm, w_uq, w_ukv, q_norm_g, k_norm_g, w_out):
    b, s, _ = h.shape
    hh = MLA_HEADS
    c_q, c_kv, k_rope = jnp.split(h @ w_a, [MLA_Q_LORA, MLA_Q_LORA + MLA_KV_LORA], axis=-1)
    q = (rms_norm(c_q, q_a_norm) @ w_uq).reshape(b, s, hh, MLA_NOPE + MLA_ROPE)
    kv = (rms_norm(c_kv, kv_a_norm) @ w_ukv).reshape(b, s, hh, MLA_NOPE + MLA_V)
    k_nope, v = jnp.split(kv, [MLA_NOPE], axis=-1)
    cos, sin = rope_tables(s, h.dtype)
    q_nope = rms_norm(q[..., :MLA_NOPE], q_norm_g[:MLA_NOPE])
    q_rope = apply_rope(rms_norm(q[..., MLA_NOPE:], q_norm_g[MLA_NOPE:]), cos[:, None, :], sin[:, None, :])
    k_nope = rms_norm(k_nope, k_norm_g[:MLA_NOPE])
    k_rope = apply_rope(rms_norm(k_rope, k_norm_g[MLA_NOPE:]), cos, sin)
    qn, qr = q_nope.transpose(0, 2, 1, 3), q_rope.transpose(0, 2, 1, 3)
    kn, v = k_nope.transpose(0, 2, 1, 3), v.transpose(0, 2, 1, 3)
    scale = (MLA_NOPE + MLA_ROPE) ** -0.5
    pos = jnp.arange(s)
    outs = []
    for blk in range(s // Q_BLOCK):
        q0, q1 = blk * Q_BLOCK, (blk + 1) * Q_BLOCK
        qp, kp = pos[q0:q1], pos[:q1]
        logits = (jnp.einsum('bhqd,bhkd->bhqk', qn[:, :, q0:q1], kn[:, :, :q1])
                  + jnp.einsum('bhqr,bkr->bhqk', qr[:, :, q0:q1], k_rope[:, :q1])).astype(jnp.float32) * scale
        logits = jnp.where(chunk_causal_mask(qp, kp), logits, MASK_VALUE)
        p = jax.nn.softmax(logits, axis=-1)
        outs.append(jnp.einsum('bhqk,bhkv->bhqv', p.astype(v.dtype), v[:, :, :q1]))
    o = jnp.concatenate(outs, axis=2).transpose(0, 2, 1, 3).reshape(b, s, hh * MLA_V)
    return o @ w_out


def setup_inputs(seed: int = 0) -> dict:
    key = jax.random.key(seed)
    keys = jax.random.split(key, 40)
    counter = [0]

    def take():
        k = keys[counter[0]]
        counter[0] += 1
        return k

    def w(shape, fan_in):
        return jax.random.normal(take(), shape, jnp.float32) * fan_in ** -0.5

    def gain(shape):
        return 1.0 + 0.01 * jax.random.normal(take(), shape, jnp.float32)

    def small(shape, scale=0.01):
        return scale * jax.random.normal(take(), shape, jnp.float32)

    d = D_MODEL
    diff_in = DIFF_HEADS * (4 * DIFF_QK_DIM + DIFF_V_DIM)
    return {
        "x": jax.random.normal(take(), (BATCH, SEQ, d), jnp.float32),
        "rel_bias_table": small((REL_BUCKETS, DIFF_HEADS), 0.2),
        "ffn_norm": gain((DEPTH, 2, d)),
        "ffn_w_in": w((DEPTH, 2, d, 2 * D_FF), d),
        "ffn_w_out": w((DEPTH, 2, D_FF, d), D_FF),
        "mixer_norm": gain((DEPTH, d)),
        "conv_w_in": w((N_CONV_LAYERS, d, 2 * d), d),
        "conv_b_in": small((N_CONV_LAYERS, 2 * d)),
        "conv_w_dw": w((N_CONV_LAYERS, CONV_WIDTH, d), CONV_WIDTH),
        "conv_b_dw": small((N_CONV_LAYERS, d)),
        "conv_ln_g": gain((N_CONV_LAYERS, d)),
        "conv_ln_b": small((N_CONV_LAYERS, d)),
        "conv_w_out": w((N_CONV_LAYERS, d, d), d),
        "conv_b_out": small((N_CONV_LAYERS, d)),
        "diff_w_in": w((N_DIFF_LAYERS, d, diff_in), d),
        "diff_q_norm": gain((N_DIFF_LAYERS, DIFF_QK_DIM)),
        "diff_k_norm": gain((N_DIFF_LAYERS, DIFF_QK_DIM)),
        "diff_lambda": small((N_DIFF_LAYERS, 4, DIFF_QK_DIM), 0.1),
        "diff_sub_norm": gain((N_DIFF_LAYERS, DIFF_V_DIM)),
        "diff_w_out": w((N_DIFF_LAYERS, DIFF_HEADS * DIFF_V_DIM, d), DIFF_HEADS * DIFF_V_DIM),
        "sb_w_in": w((N_SB_LAYERS, d, 3 * SB_HEADS * SB_HEAD_DIM), d),
        "sb_w_out": w((N_SB_LAYERS, SB_HEADS * SB_HEAD_DIM, d), SB_HEADS * SB_HEAD_DIM),
        "mla_w_a": w((N_MLA_LAYERS, d, MLA_Q_LORA + MLA_KV_LORA + MLA_ROPE), d),
        "mla_q_a_norm": gain((N_MLA_LAYERS, MLA_Q_LORA)),
        "mla_kv_a_norm": gain((N_MLA_LAYERS, MLA_KV_LORA)),
        "mla_w_uq": w((N_MLA_LAYERS, MLA_Q_LORA, MLA_HEADS * (MLA_NOPE + MLA_ROPE)), MLA_Q_LORA),
        "mla_w_ukv": w((N_MLA_LAYERS, MLA_KV_LORA, MLA_HEADS * (MLA_NOPE + MLA_V)), MLA_KV_LORA),
        "mla_q_norm": gain((N_MLA_LAYERS, MLA_NOPE + MLA_ROPE)),
        "mla_k_norm": gain((N_MLA_LAYERS, MLA_NOPE + MLA_ROPE)),
        "mla_w_out": w((N_MLA_LAYERS, MLA_HEADS * MLA_V, d), MLA_HEADS * MLA_V),
    }


def reference(x, rel_bias_table, ffn_norm, ffn_w_in, ffn_w_out, mixer_norm,
              conv_w_in, conv_b_in, conv_w_dw, conv_b_dw, conv_ln_g, conv_ln_b, conv_w_out, conv_b_out,
              diff_w_in, diff_q_norm, diff_k_norm, diff_lambda, diff_sub_norm, diff_w_out,
              sb_w_in, sb_w_out,
              mla_w_a, mla_q_a_norm, mla_kv_a_norm, mla_w_uq, mla_w_ukv, mla_q_norm, mla_k_norm, mla_w_out):
    for i in range(DEPTH):
        mixer, j = i % N_MIXERS, i // N_MIXERS
        x = x + 0.5 * swiglu_ffn(x, ffn_norm[i, 0], ffn_w_in[i, 0], ffn_w_out[i, 0])
        h = rms_norm(x, mixer_norm[i])
        if mixer == 0:
            y = conformer_conv_module(h, conv_w_in[j], conv_b_in[j], conv_w_dw[j], conv_b_dw[j],
                                      conv_ln_g[j], conv_ln_b[j], conv_w_out[j], conv_b_out[j])
        elif mixer == 1:
            y = differential_attention(h, i, diff_w_in[j], diff_q_norm[j], diff_k_norm[j], diff_lambda[j],
                                       diff_sub_norm[j], diff_w_out[j], rel_bias_table)
        elif mixer == 2:
            y = stick_breaking_attention(h, sb_w_in[j], sb_w_out[j])
        else:
            y = latent_attention(h, mla_w_a[j], mla_q_a_norm[j], mla_kv_a_norm[j], mla_w_uq[j],
                                 mla_w_ukv[j], mla_q_norm[j], mla_k_norm[j], mla_w_out[j])
        x = x + y
        x = x + 0.5 * swiglu_ffn(x, ffn_norm[i, 1], ffn_w_in[i, 1], ffn_w_out[i, 1])
    return x
```

```python
import functools
import math

import jax
import jax.numpy as jnp
from jax import lax
from jax.experimental import pallas as pl
from jax.experimental.pallas import tpu as pltpu

F32 = jnp.float32
BF16 = jnp.bfloat16

D_MODEL = 1024
DEPTH = 4
CHUNK = 64
N_MIXERS = 4
D_FF = 2816
RMS_EPS = 1e-6
LN_EPS = 1e-5
MASK_VALUE = -1e30
CONV_WIDTH = 31
DIFF_HEADS = 8
DIFF_QK_DIM = 64
DIFF_V_DIM = 128
LAMBDA_INIT_BASE = 0.8
LAMBDA_INIT_SCALE = 0.6
LAMBDA_INIT_DECAY = 0.3
REL_BUCKETS = 32
REL_MAX_DIST = 128
SB_HEADS = 16
SB_HEAD_DIM = 64
MLA_HEADS = 16
MLA_Q_LORA = 384
MLA_KV_LORA = 256
MLA_NOPE = 64
MLA_ROPE = 32
MLA_V = 64
ROPE_BASE = 10000.0

LANES = 128
TOK_TILE = 512
FF_CHUNK = 256
SB_KEY_BLOCK = 128
CONV_HALO = 32
CONV_ROWS = 32
VMEM_LIMIT = 56 * 1024 * 1024
F32_EXP_ZERO = -104.0


def _params(sem):
    return pltpu.CompilerParams(dimension_semantics=sem, vmem_limit_bytes=VMEM_LIMIT)


def _const_spec(shape):
    nd = len(shape)
    return pl.BlockSpec(shape, lambda *_: (0,) * nd, pipeline_mode=pl.Buffered(1))


def _rms_bf16(x, g):
    ms = jnp.mean(x * x, axis=-1, keepdims=True)
    return (x * lax.rsqrt(ms + RMS_EPS) * g).astype(BF16)


def _split_dot(x, w):
    hi = x.astype(BF16)
    lo = (x - hi.astype(F32)).astype(BF16)
    return (jnp.dot(hi, w, preferred_element_type=F32)
            + jnp.dot(lo, w, preferred_element_type=F32))


def _dot_nt(a, b):
    return lax.dot_general(a, b, (((1,), (1,)), ((), ())), preferred_element_type=F32)


def _ffn_kernel(x_ref, g_ref, win_ref, wout_ref, o_ref, gate_ref):
    x = x_ref[...]
    h = _rms_bf16(x, g_ref[...])
    for c in range(D_FF // FF_CHUNK):
        lo, hi = c * FF_CHUNK, (c + 1) * FF_CHUNK
        a = jnp.dot(h, win_ref[:, lo:hi], preferred_element_type=F32)
        u = jnp.dot(h, win_ref[:, D_FF + lo:D_FF + hi], preferred_element_type=F32)
        gate_ref[:, lo:hi] = (a * jax.nn.sigmoid(a) * u).astype(BF16)
    y = jnp.dot(gate_ref[...], wout_ref[...], preferred_element_type=F32)
    o_ref[...] = x + 0.5 * y


def _ffn(x2, g, w_in, w_out):
    n, d = x2.shape
    tm = TOK_TILE
    return pl.pallas_call(
        _ffn_kernel,
        grid=(n // tm,),
        in_specs=[pl.BlockSpec((tm, d), lambda i: (i, 0)),
                  _const_spec((1, d)),
                  _const_spec((d, 2 * D_FF)),
                  _const_spec((D_FF, d))],
        out_specs=pl.BlockSpec((tm, d), lambda i: (i, 0)),
        out_shape=jax.ShapeDtypeStruct((n, d), F32),
        scratch_shapes=[pltpu.VMEM((tm, D_FF), BF16)],
        compiler_params=_params(("parallel",)),
        name="ffn",
    )(x2, g.reshape(1, d), w_in.astype(BF16), w_out.astype(BF16))


def _out_proj_kernel(x_ref, o_ref, w_ref, y_ref):
    y_ref[...] = x_ref[...] + jnp.dot(o_ref[...], w_ref[...], preferred_element_type=F32)


def _out_proj(x2, o2, w):
    n, d = x2.shape
    kd = o2.shape[1]
    tm = TOK_TILE
    return pl.pallas_call(
        _out_proj_kernel,
        grid=(n // tm,),
        in_specs=[pl.BlockSpec((tm, d), lambda i: (i, 0)),
                  pl.BlockSpec((tm, kd), lambda i: (i, 0)),
                  _const_spec((kd, d))],
        out_specs=pl.BlockSpec((tm, d), lambda i: (i, 0)),
        out_shape=jax.ShapeDtypeStruct((n, d), F32),
        compiler_params=_params(("parallel",)),
        name="out_proj",
    )(x2, o2, w.astype(BF16))


def _conv_in_kernel(x_ref, g_ref, w_ref, b_ref, u_ref):
    d = D_MODEL
    h = _rms_bf16(x_ref[...], g_ref[...])
    y = jnp.dot(h, w_ref[...], preferred_element_type=F32) + b_ref[...]
    u_ref[...] = y[:, :d] * jax.nn.sigmoid(y[:, d:])


def _conv_out_kernel(x_ref, ucur_ref, uprev_ref, wdw_ref, bdw_ref, lng_ref, lnb_ref,
                     wout_ref, bout_ref, y_ref, ext_ref, conv_ref):
    ts = ucur_ref.shape[1]
    i = pl.program_id(1)
    ext_ref[0:CONV_HALO, :] = jnp.where(i > 0, uprev_ref[0], 0.0)
    ext_ref[CONV_HALO:, :] = ucur_ref[0]
    off = CONV_HALO - (CONV_WIDTH - 1)
    for r in range(ts // CONV_ROWS):
        r0 = r * CONV_ROWS
        acc = ext_ref[r0 + off:r0 + off + CONV_ROWS, :] * wdw_ref[0:1, :]
        for k in range(1, CONV_WIDTH):
            acc = acc + ext_ref[r0 + off + k:r0 + off + k + CONV_ROWS, :] * wdw_ref[k:k + 1, :]
        conv_ref[r0:r0 + CONV_ROWS, :] = acc + bdw_ref[...]
    c = conv_ref[...]
    mu = jnp.mean(c, axis=-1, keepdims=True)
    cc = c - mu
    var = jnp.mean(cc * cc, axis=-1, keepdims=True)
    ln = cc * lax.rsqrt(var + LN_EPS) * lng_ref[...] + lnb_ref[...]
    act = (ln * jax.nn.sigmoid(ln)).astype(BF16)
    y = jnp.dot(act, wout_ref[...], preferred_element_type=F32) + bout_ref[...]
    y_ref[0] = x_ref[0] + y


def _conv_mixer(x3, g, w_in, b_in, w_dw, b_dw, ln_g, ln_b, w_out, b_out):
    b, s, d = x3.shape
    n = b * s
    tm = TOK_TILE
    u = pl.pallas_call(
        _conv_in_kernel,
        grid=(n // tm,),
        in_specs=[pl.BlockSpec((tm, d), lambda i: (i, 0)),
                  _const_spec((1, d)),
                  _const_spec((d, 2 * d)),
                  _const_spec((1, 2 * d))],
        out_specs=pl.BlockSpec((tm, d), lambda i: (i, 0)),
        out_shape=jax.ShapeDtypeStruct((n, d), F32),
        compiler_params=_params(("parallel",)),
        name="conv_in",
    )(x3.reshape(n, d), g.reshape(1, d), w_in.astype(BF16), b_in.reshape(1, 2 * d))
    u3 = u.reshape(b, s, d)
    halo_per_tile = tm // CONV_HALO
    row = lambda v: v.reshape(1, d)
    return pl.pallas_call(
        _conv_out_kernel,
        grid=(b, s // tm),
        in_specs=[pl.BlockSpec((1, tm, d), lambda bi, i: (bi, i, 0)),
                  pl.BlockSpec((1, tm, d), lambda bi, i: (bi, i, 0)),
                  pl.BlockSpec((1, CONV_HALO, d),
                               lambda bi, i: (bi, jnp.maximum(i * halo_per_tile - 1, 0), 0)),
                  _const_spec((CONV_WIDTH, d)),
                  _const_spec((1, d)), _const_spec((1, d)), _const_spec((1, d)),
                  _const_spec((d, d)),
                  _const_spec((1, d))],
        out_specs=pl.BlockSpec((1, tm, d), lambda bi, i: (bi, i, 0)),
        out_shape=jax.ShapeDtypeStruct((b, s, d), F32),
        scratch_shapes=[pltpu.VMEM((tm + CONV_HALO, d), F32), pltpu.VMEM((tm, d), F32)],
        compiler_params=_params(("parallel", "parallel")),
        name="conv_out",
    )(x3, u3, u3, w_dw, row(b_dw), row(ln_g), row(ln_b), w_out.astype(BF16), row(b_out))


def _diff_proj_kernel(x_ref, g_ref, wq_ref, wkt_ref, wv_ref, gq_ref, gk_ref, gmat_ref,
                      q_ref, kt_ref, v_ref):
    h = _rms_bf16(x_ref[0], g_ref[...])
    yq = jnp.dot(h, wq_ref[...], preferred_element_type=F32)
    width = gmat_ref.shape[0]
    for c in range(yq.shape[1] // width):
        sl = slice(c * width, (c + 1) * width)
        y = yq[:, sl]
        ms = _split_dot(y * y, gmat_ref[...])
        q_ref[0, :, sl] = (y * lax.rsqrt(ms + RMS_EPS) * gq_ref[:, sl]).astype(BF16)
    ykt = _dot_nt(wkt_ref[...], h)
    for r in range(ykt.shape[0] // DIFF_QK_DIM):
        sl = slice(r * DIFF_QK_DIM, (r + 1) * DIFF_QK_DIM)
        y = ykt[sl, :]
        ms = jnp.mean(y * y, axis=0, keepdims=True)
        kt_ref[0, 0, sl, :] = (y * lax.rsqrt(ms + RMS_EPS) * gk_ref[sl, :]).astype(BF16)
    v_ref[0] = jnp.dot(h, wv_ref[...], preferred_element_type=F32).astype(BF16)


def _softmax_step(s, v, m_ref, l_ref, acc_ref):
    m_prev = m_ref[...]
    m_new = jnp.maximum(m_prev, jnp.max(s, axis=-1, keepdims=True))
    alpha = jnp.exp(m_prev - m_new)
    p = jnp.exp(s - m_new)
    l_ref[...] = alpha * l_ref[...] + jnp.sum(p, axis=-1, keepdims=True)
    acc_ref[...] = alpha * acc_ref[...] + jnp.dot(p.astype(BF16), v, preferred_element_type=F32)
    m_ref[...] = m_new


def _diff_attn_kernel(tab_ref, scal_ref, q_ref, kt_ref, v_ref, bucket_ref, subg_ref, o_ref,
                      bias_ref, m_ref, l_ref, acc_ref):
    t = q_ref.shape[1]
    hd = pl.program_id(1)
    i = pl.program_id(2)

    @pl.when(i == 0)
    def _build_bias():
        far = tab_ref[REL_BUCKETS // 2 - 1, hd]
        rows = lax.broadcasted_iota(jnp.int32, (t, t), 0)
        cols = lax.broadcasted_iota(jnp.int32, (t, t), 1)
        visible = (cols // CHUNK) <= (rows // CHUNK)
        for tile in range(2):
            bk = bucket_ref[tile]
            bias = jnp.zeros((t, t), F32)
            for b in range(REL_BUCKETS):
                bias = jnp.where(bk == b, tab_ref[b, hd] - far, bias)
            if tile == 1:
                bias = jnp.where(visible, bias, MASK_VALUE)
            bias_ref[tile] = bias

    m_ref[...] = jnp.full(m_ref.shape, -jnp.inf, F32)
    l_ref[...] = jnp.zeros(l_ref.shape, F32)
    acc_ref[...] = jnp.zeros(acc_ref.shape, F32)

    q2 = q_ref[0]
    lane = lax.broadcasted_iota(jnp.int32, q2.shape, 1)
    qmaps = [jnp.where((lane // DIFF_QK_DIM) == mp, q2, jnp.zeros_like(q2)) for mp in range(2)]

    def step(j, bias_tile):
        kt = kt_ref[0, j]
        v = v_ref[0, pl.ds(pl.multiple_of(j * t, t), t), :]
        for mp in range(2):
            s = jnp.dot(qmaps[mp], kt, preferred_element_type=F32)
            if bias_tile is not None:
                s = s + bias_ref[bias_tile]
            _softmax_step(s, v, m_ref.at[mp], l_ref.at[mp], acc_ref.at[mp])

    def far_body(j, carry):
        step(j, None)
        return carry

    lax.fori_loop(0, i - 1, far_body, 0)

    @pl.when(i >= 1)
    def _prev():
        step(i - 1, 0)

    step(i, 1)

    lam = scal_ref[0]
    attn = acc_ref[0] / l_ref[0] - lam * (acc_ref[1] / l_ref[1])
    ms = jnp.mean(attn * attn, axis=-1, keepdims=True)
    o_ref[0] = (attn * lax.rsqrt(ms + RMS_EPS) * subg_ref[...]).astype(BF16)


def _t5_bucket(rel):
    nb = REL_BUCKETS // 2
    bucket = jnp.where(rel > 0, nb, 0)
    n = jnp.abs(rel)
    max_exact = nb // 2
    n_f = jnp.maximum(n, 1).astype(jnp.float32)
    large = max_exact + (jnp.log(n_f / max_exact) / math.log(REL_MAX_DIST / max_exact)
                         * (nb - max_exact)).astype(jnp.int32)
    large = jnp.minimum(large, nb - 1)
    return bucket + jnp.where(n < max_exact, n, large)


def _group_mean_matrix(groups, width):
    idx = []
    for gi, size in enumerate(groups):
        idx += [gi] * size
    reps = width // len(idx)
    gid = jnp.asarray([r * len(groups) + g for r in range(reps) for g in idx], jnp.int32)
    sizes = jnp.asarray([float(groups[g]) for _ in range(reps) for g in idx], F32)
    same = gid[:, None] == gid[None, :]
    return jnp.where(same, 1.0 / sizes[None, :], 0.0).astype(BF16)


def _diff_mixer(x3, layer_idx, g, w_in, q_norm_g, k_norm_g, lam, sub_norm_g, w_out, rel_table):
    b, s, d = x3.shape
    t = TOK_TILE
    nkb = s // t
    hh, dq, dv = DIFF_HEADS, DIFF_QK_DIM, DIFF_V_DIM
    w3 = w_in.reshape(d, hh, 4 * dq + dv)
    wq = w3[:, :, :2 * dq].reshape(d, hh * 2 * dq).astype(BF16)
    wkt = w3[:, :, 2 * dq:4 * dq].reshape(d, hh * 2 * dq).T.astype(BF16)
    wv = w3[:, :, 4 * dq:].reshape(d, hh * dv).astype(BF16)
    scale = dq ** -0.5
    gq = (jnp.tile(q_norm_g, 2 * hh) * scale).reshape(1, hh * 2 * dq)
    gk = jnp.tile(k_norm_g, 2 * hh).reshape(hh * 2 * dq, 1)
    gmat = _group_mean_matrix([dq], 2 * LANES)
    q, kt, v = pl.pallas_call(
        _diff_proj_kernel,
        grid=(b, nkb),
        in_specs=[pl.BlockSpec((1, t, d), lambda bi, i: (bi, i, 0)),
                  _const_spec((1, d)),
                  _const_spec((d, hh * 2 * dq)),
                  _const_spec((hh * 2 * dq, d)),
                  _const_spec((d, hh * dv)),
                  _const_spec((1, hh * 2 * dq)),
                  _const_spec((hh * 2 * dq, 1)),
                  _const_spec((2 * LANES, 2 * LANES))],
        out_specs=[pl.BlockSpec((1, t, hh * 2 * dq), lambda bi, i: (bi, i, 0)),
                   pl.BlockSpec((1, 1, hh * 2 * dq, t), lambda bi, i: (bi, i, 0, 0)),
                   pl.BlockSpec((1, t, hh * dv), lambda bi, i: (bi, i, 0))],
        out_shape=[jax.ShapeDtypeStruct((b, s, hh * 2 * dq), BF16),
                   jax.ShapeDtypeStruct((b, nkb, hh * 2 * dq, t), BF16),
                   jax.ShapeDtypeStruct((b, s, hh * dv), BF16)],
        compiler_params=_params(("parallel", "parallel")),
        name="diff_proj",
    )(x3, g.reshape(1, d), wq, wkt, wv, gq, gk, gmat)

    lam_init = LAMBDA_INIT_BASE - LAMBDA_INIT_SCALE * math.exp(-LAMBDA_INIT_DECAY * layer_idx)
    lam_full = (jnp.exp(jnp.sum(lam[0] * lam[1]).astype(F32))
                - jnp.exp(jnp.sum(lam[2] * lam[3]).astype(F32)) + lam_init)
    scal = jnp.reshape(lam_full, (1,)).astype(F32)
    r = jnp.arange(t)
    rel_diag = r[None, :] - r[:, None]
    buckets = jnp.stack([_t5_bucket(rel_diag - t), _t5_bucket(rel_diag)]).astype(jnp.int32)
    subg = (sub_norm_g * (1.0 - lam_init)).reshape(1, dv)
    smem = pl.BlockSpec(memory_space=pltpu.SMEM)
    o = pl.pallas_call(
        _diff_attn_kernel,
        grid=(b, hh, nkb),
        in_specs=[smem, smem,
                  pl.BlockSpec((1, t, 2 * dq), lambda bi, h, i: (bi, i, h)),
                  pl.BlockSpec((1, nkb, 2 * dq, t), lambda bi, h, i: (bi, 0, h, 0)),
                  pl.BlockSpec((1, s, dv), lambda bi, h, i: (bi, 0, h)),
                  _const_spec((2, t, t)),
                  _const_spec((1, dv))],
        out_specs=pl.BlockSpec((1, t, dv), lambda bi, h, i: (bi, i, h)),
        out_shape=jax.ShapeDtypeStruct((b, s, hh * dv), BF16),
        scratch_shapes=[pltpu.VMEM((2, t, t), F32),
                        pltpu.VMEM((2, t, 1), F32),
                        pltpu.VMEM((2, t, 1), F32),
                        pltpu.VMEM((2, t, dv), F32)],
        compiler_params=_params(("parallel", "parallel", "arbitrary")),
        name="diff_attn",
    )(rel_table.astype(F32), scal, q, kt, v, buckets, subg)
    return _out_proj(x3.reshape(b * s, d), o.reshape(b * s, hh * dv), w_out).reshape(b, s, d)


def _pair_masked(v, head_dim):
    lane = lax.broadcasted_iota(jnp.int32, v.shape, 1)
    odd = (lane // head_dim) % 2 == 1
    zero = jnp.zeros_like(v)
    return jnp.where(odd, zero, v), jnp.where(odd, v, zero)


def _sb_proj_kernel(x_ref, g_ref, wq_ref, wkt_ref, wv_ref, q_ref, kt_ref, va_ref, vb_ref):
    h = _rms_bf16(x_ref[0], g_ref[...])
    q_ref[0] = (jnp.dot(h, wq_ref[...], preferred_element_type=F32)
                * (SB_HEAD_DIM ** -0.5)).astype(BF16)
    ykt = _dot_nt(wkt_ref[...], h).astype(BF16)
    kb = kt_ref.shape[3]
    for c in range(kt_ref.shape[1]):
        kt_ref[0, c] = ykt[:, c * kb:(c + 1) * kb]
    v = jnp.dot(h, wv_ref[...], preferred_element_type=F32).astype(BF16)
    va, vb = _pair_masked(v, SB_HEAD_DIM)
    va_ref[0] = va
    vb_ref[0] = vb


def _sb_attn_kernel(q_ref, kt_ref, va_ref, vb_ref, tri_ref, o_ref, run_ref, acc_ref):
    t = q_ref.shape[1]
    kb = SB_KEY_BLOCK
    i = pl.program_id(2)
    sub = t // kb
    q2 = q_ref[0]
    lane = lax.broadcasted_iota(jnp.int32, q2.shape, 1)
    qh = [jnp.where((lane // SB_HEAD_DIM) == a, q2, jnp.zeros_like(q2)) for a in range(2)]
    v_refs = (va_ref, vb_ref)
    run_ref[...] = jnp.zeros(run_ref.shape, F32)
    acc_ref[...] = jnp.zeros(acc_ref.shape, F32)

    def block(j, causal):
        kt = kt_ref[0, j]
        k0 = pl.multiple_of(j * kb, kb)
        for a in range(2):
            z = jnp.dot(qh[a], kt, preferred_element_type=F32)
            soft = jnp.log(1.0 + jnp.exp(-jnp.abs(z)))
            log_beta = jnp.minimum(z, 0.0) - soft
            log_keep = log_beta - z
            if causal is not None:
                log_keep = jnp.where(causal, log_keep, 0.0)
            hi = log_keep.astype(BF16)
            lo = (log_keep - hi.astype(F32)).astype(BF16)
            sums = jnp.dot(jnp.concatenate([hi, lo], axis=1), tri_ref[...],
                           preferred_element_type=F32)
            run = run_ref[a]
            w = jnp.exp(log_beta + sums[:, :kb] + run)
            if causal is not None:
                w = jnp.where(causal, w, 0.0)
            v = v_refs[a][0, pl.ds(k0, kb), :]
            acc_ref[...] += jnp.dot(w.astype(BF16), v, preferred_element_type=F32)
            run_ref[a] = run + sums[:, kb:]

    rows = lax.broadcasted_iota(jnp.int32, (t, kb), 0)
    cols = lax.broadcasted_iota(jnp.int32, (t, kb), 1)
    for jj in range(sub - 1, -1, -1):
        block(i * sub + jj, cols + jj * kb < rows)

    def cond(c):
        j, top = c
        return jnp.logical_and(j >= 0, top > F32_EXP_ZERO)

    def body(c):
        j, _ = c
        block(j, None)
        return j - 1, jnp.max(run_ref[...])

    lax.while_loop(cond, body, (i * sub - 1, jnp.max(run_ref[...])))
    o_ref[0] = acc_ref[...].astype(BF16)


def _sb_mixer(x3, g, w_in, w_out):
    b, s, d = x3.shape
    t = TOK_TILE
    kb = SB_KEY_BLOCK
    hh, dh = SB_HEADS, SB_HEAD_DIM
    w3 = w_in.reshape(d, hh, 3 * dh)
    wq = w3[:, :, :dh].reshape(d, hh * dh).astype(BF16)
    wkt = w3[:, :, dh:2 * dh].reshape(d, hh * dh).T.astype(BF16)
    wv = w3[:, :, 2 * dh:].reshape(d, hh * dh).astype(BF16)
    tok = lambda: pl.BlockSpec((1, t, hh * dh), lambda bi, i: (bi, i, 0))
    q, kt, va, vb = pl.pallas_call(
        _sb_proj_kernel,
        grid=(b, s // t),
        in_specs=[pl.BlockSpec((1, t, d), lambda bi, i: (bi, i, 0)),
                  _const_spec((1, d)),
                  _const_spec((d, hh * dh)),
                  _const_spec((hh * dh, d)),
                  _const_spec((d, hh * dh))],
        out_specs=[tok(),
                   pl.BlockSpec((1, t // kb, hh * dh, kb), lambda bi, i: (bi, i, 0, 0)),
                   tok(), tok()],
        out_shape=[jax.ShapeDtypeStruct((b, s, hh * dh), BF16),
                   jax.ShapeDtypeStruct((b, s // kb, hh * dh, kb), BF16),
                   jax.ShapeDtypeStruct((b, s, hh * dh), BF16),
                   jax.ShapeDtypeStruct((b, s, hh * dh), BF16)],
        compiler_params=_params(("parallel", "parallel")),
        name="sb_proj",
    )(x3, g.reshape(1, d), wq, wkt, wv)

    sk = jnp.arange(2 * kb) % kb
    cj = jnp.arange(2 * kb)
    tri = jnp.where((cj[None, :] >= kb) | (sk[:, None] > cj[None, :]), 1.0, 0.0).astype(BF16)
    pair = 2 * dh
    o = pl.pallas_call(
        _sb_attn_kernel,
        grid=(b, hh // 2, s // t),
        in_specs=[pl.BlockSpec((1, t, pair), lambda bi, h, i: (bi, i, h)),
                  pl.BlockSpec((1, s // kb, pair, kb), lambda bi, h, i: (bi, 0, h, 0)),
                  pl.BlockSpec((1, s, pair), lambda bi, h, i: (bi, 0, h)),
                  pl.BlockSpec((1, s, pair), lambda bi, h, i: (bi, 0, h)),
                  _const_spec((2 * kb, 2 * kb))],
        out_specs=pl.BlockSpec((1, t, pair), lambda bi, h, i: (bi, i, h)),
        out_shape=jax.ShapeDtypeStruct((b, s, hh * dh), BF16),
        scratch_shapes=[pltpu.VMEM((2, t, kb), F32), pltpu.VMEM((t, pair), F32)],
        compiler_params=_params(("parallel", "parallel", "arbitrary")),
        name="sb_attn",
    )(q, kt, va, vb, tri)
    return _out_proj(x3.reshape(b * s, d), o.reshape(b * s, hh * dh), w_out).reshape(b, s, d)


MLA_QK_PAD = MLA_NOPE + 2 * MLA_ROPE


def _mla_proj_kernel(x_ref, g_ref, waq_ref, wakv_ref, wart_ref, gqa_ref, gkva_ref, wuq_ref,
                     wukt_ref, wuv_ref, gmat_ref, gq_ref, mq_ref, gkn_ref, gkr_ref, cst_ref,
                     q_ref, kt_ref, va_ref, vb_ref):
    h = _rms_bf16(x_ref[0], g_ref[...])
    cq = _rms_bf16(jnp.dot(h, waq_ref[...], preferred_element_type=F32), gqa_ref[...])
    ckv = _rms_bf16(jnp.dot(h, wakv_ref[...], preferred_element_type=F32), gkva_ref[...])
    yq = jnp.dot(cq, wuq_ref[...], preferred_element_type=F32)
    width = gmat_ref.shape[0]
    for c in range(yq.shape[1] // width):
        sl = slice(c * width, (c + 1) * width)
        y = yq[:, sl]
        ms = _split_dot(y * y, gmat_ref[...])
        q_ref[0, :, sl] = (y * lax.rsqrt(ms + RMS_EPS) * gq_ref[:, sl] * mq_ref[...]).astype(BF16)
    yr = _dot_nt(wart_ref[...], h)
    msr = jnp.mean(yr[:MLA_ROPE] * yr[:MLA_ROPE], axis=0, keepdims=True)
    rn = yr * lax.rsqrt(msr + RMS_EPS) * gkr_ref[...]
    kf = (rn[:MLA_ROPE] * cst_ref[:MLA_ROPE, :] + rn[MLA_ROPE:] * cst_ref[MLA_ROPE:, :]).astype(BF16)
    ykt = _dot_nt(wukt_ref[...], ckv)
    for hd in range(MLA_HEADS):
        y = ykt[hd * MLA_NOPE:(hd + 1) * MLA_NOPE, :]
        ms = jnp.mean(y * y, axis=0, keepdims=True)
        base = hd * MLA_QK_PAD
        kt_ref[0, 0, base:base + MLA_NOPE, :] = (y * lax.rsqrt(ms + RMS_EPS) * gkn_ref[...]).astype(BF16)
        kt_ref[0, 0, base + MLA_NOPE:base + MLA_NOPE + MLA_ROPE, :] = kf
        kt_ref[0, 0, base + MLA_NOPE + MLA_ROPE:base + MLA_QK_PAD, :] = kf
    v = jnp.dot(ckv, wuv_ref[...], preferred_element_type=F32).astype(BF16)
    va, vb = _pair_masked(v, MLA_V)
    va_ref[0] = va
    vb_ref[0] = vb


def _mla_attn_kernel(q_ref, kt_ref, va_ref, vb_ref, o_ref, m_ref, l_ref, acc_ref):
    t = q_ref.shape[1]
    i = pl.program_id(2)
    m_ref[...] = jnp.full(m_ref.shape, -jnp.inf, F32)
    l_ref[...] = jnp.zeros(l_ref.shape, F32)
    acc_ref[...] = jnp.zeros(acc_ref.shape, F32)
    lane = lax.broadcasted_iota(jnp.int32, (t, 2 * MLA_V), 1)
    first = lane < MLA_V
    v_refs = (va_ref, vb_ref)

    def step(j, visible):
        kt = kt_ref[0, j]
        k0 = pl.multiple_of(j * t, t)
        alphas, pvs = [], []
        for a in range(2):
            sl = slice(a * MLA_QK_PAD, (a + 1) * MLA_QK_PAD)
            s = jnp.dot(q_ref[0, :, sl], kt[sl, :], preferred_element_type=F32)
            if visible is not None:
                s = jnp.where(visible, s, MASK_VALUE)
            m_prev = m_ref[a]
            m_new = jnp.maximum(m_prev, jnp.max(s, axis=-1, keepdims=True))
            alpha = jnp.exp(m_prev - m_new)
            p = jnp.exp(s - m_new)
            l_ref[a] = alpha * l_ref[a] + jnp.sum(p, axis=-1, keepdims=True)
            m_ref[a] = m_new
            alphas.append(alpha)
            pvs.append(jnp.dot(p.astype(BF16), v_refs[a][0, pl.ds(k0, t), :],
                               preferred_element_type=F32))
        acc_ref[...] = jnp.where(first, alphas[0], alphas[1]) * acc_ref[...] + pvs[0] + pvs[1]

    def body(j, carry):
        step(j, None)
        return carry

    lax.fori_loop(0, i, body, 0)
    rows = lax.broadcasted_iota(jnp.int32, (t, t), 0)
    cols = lax.broadcasted_iota(jnp.int32, (t, t), 1)
    step(i, (cols // CHUNK) <= (rows // CHUNK))
    o_ref[0] = (acc_ref[...] / jnp.where(first, l_ref[0], l_ref[1])).astype(BF16)


def _rotate_half_cols(w):
    half = MLA_ROPE // 2
    return jnp.concatenate([-w[..., half:], w[..., :half]], axis=-1)


def _mla_mixer(x3, g, w_a, q_a_norm, kv_a_norm, w_uq, w_ukv, q_norm_g, k_norm_g, w_out):
    b, s, d = x3.shape
    t = TOK_TILE
    nkb = s // t
    hh = MLA_HEADS
    half = MLA_ROPE // 2
    qp = MLA_QK_PAD
    wa_q = w_a[:, :MLA_Q_LORA].astype(BF16)
    wa_kv = w_a[:, MLA_Q_LORA:MLA_Q_LORA + MLA_KV_LORA].astype(BF16)
    wa_r = w_a[:, MLA_Q_LORA + MLA_KV_LORA:]
    wart = jnp.concatenate([wa_r, _rotate_half_cols(wa_r)], axis=1).T.astype(BF16)
    wq3 = w_uq.reshape(MLA_Q_LORA, hh, MLA_NOPE + MLA_ROPE)
    wq_r = wq3[:, :, MLA_NOPE:]
    wuq = jnp.concatenate([wq3, _rotate_half_cols(wq_r)], axis=2).reshape(MLA_Q_LORA, hh * qp).astype(BF16)
    wkv3 = w_ukv.reshape(MLA_KV_LORA, hh, MLA_NOPE + MLA_V)
    wukt = wkv3[:, :, :MLA_NOPE].reshape(MLA_KV_LORA, hh * MLA_NOPE).T.astype(BF16)
    wuv = wkv3[:, :, MLA_NOPE:].reshape(MLA_KV_LORA, hh * MLA_V).astype(BF16)
    swap = lambda v: jnp.concatenate([v[half:], v[:half]])
    scale = (MLA_NOPE + MLA_ROPE) ** -0.5
    gq_head = jnp.concatenate([q_norm_g, swap(q_norm_g[MLA_NOPE:])]) * scale
    gq = jnp.tile(gq_head, hh).reshape(1, hh * qp)
    gkn = k_norm_g[:MLA_NOPE].reshape(MLA_NOPE, 1)
    gkr = jnp.concatenate([k_norm_g[MLA_NOPE:], swap(k_norm_g[MLA_NOPE:])]).reshape(2 * MLA_ROPE, 1)
    inv = ROPE_BASE ** (-jnp.arange(0, MLA_ROPE, 2, dtype=F32) / MLA_ROPE)
    ang = jnp.arange(s, dtype=F32)[:, None] * inv[None, :]
    cos, sin = jnp.cos(ang), jnp.sin(ang)
    cs = jnp.concatenate([cos, cos, sin, sin], axis=1)
    mq_head = jnp.concatenate([jnp.ones((s, MLA_NOPE), F32), cs], axis=1)
    mq = jnp.tile(mq_head, (1, 2))
    gmat = _group_mean_matrix([MLA_NOPE, MLA_ROPE, MLA_ROPE], 2 * qp)
    tok = lambda w: pl.BlockSpec((1, t, w), lambda bi, i: (bi, i, 0))
    q, kt, va, vb = pl.pallas_call(
        _mla_proj_kernel,
        grid=(b, nkb),
        in_specs=[tok(d),
                  _const_spec((1, d)),
                  _const_spec((d, MLA_Q_LORA)),
                  _const_spec((d, MLA_KV_LORA)),
                  _const_spec((2 * MLA_ROPE, d)),
                  _const_spec((1, MLA_Q_LORA)),
                  _const_spec((1, MLA_KV_LORA)),
                  _const_spec((MLA_Q_LORA, hh * qp)),
                  _const_spec((hh * MLA_NOPE, MLA_KV_LORA)),
                  _const_spec((MLA_KV_LORA, hh * MLA_V)),
                  _const_spec((2 * qp, 2 * qp)),
                  _const_spec((1, hh * qp)),
                  pl.BlockSpec((t, 2 * qp), lambda bi, i: (i, 0)),
                  _const_spec((MLA_NOPE, 1)),
                  _const_spec((2 * MLA_ROPE, 1)),
                  pl.BlockSpec((2 * MLA_ROPE, t), lambda bi, i: (0, i))],
        out_specs=[tok(hh * qp),
                   pl.BlockSpec((1, 1, hh * qp, t), lambda bi, i: (bi, i, 0, 0)),
                   tok(hh * MLA_V), tok(hh * MLA_V)],
        out_shape=[jax.ShapeDtypeStruct((b, s, hh * qp), BF16),
                   jax.ShapeDtypeStruct((b, nkb, hh * qp, t), BF16),
                   jax.ShapeDtypeStruct((b, s, hh * MLA_V), BF16),
                   jax.ShapeDtypeStruct((b, s, hh * MLA_V), BF16)],
        compiler_params=_params(("parallel", "parallel")),
        name="mla_proj",
    )(x3, g.reshape(1, d), wa_q, wa_kv, wart, q_a_norm.reshape(1, -1), kv_a_norm.reshape(1, -1),
      wuq, wukt, wuv, gmat, gq, mq, gkn, gkr, cs.T)

    pair_q, pair_v = 2 * qp, 2 * MLA_V
    o = pl.pallas_call(
        _mla_attn_kernel,
        grid=(b, hh // 2, nkb),
        in_specs=[pl.BlockSpec((1, t, pair_q), lambda bi, h, i: (bi, i, h)),
                  pl.BlockSpec((1, nkb, pair_q, t), lambda bi, h, i: (bi, 0, h, 0)),
                  pl.BlockSpec((1, s, pair_v), lambda bi, h, i: (bi, 0, h)),
                  pl.BlockSpec((1, s, pair_v), lambda bi, h, i: (bi, 0, h))],
        out_specs=pl.BlockSpec((1, t, pair_v), lambda bi, h, i: (bi, i, h)),
        out_shape=jax.ShapeDtypeStruct((b, s, hh * MLA_V), BF16),
        scratch_shapes=[pltpu.VMEM((2, t, 1), F32), pltpu.VMEM((2, t, 1), F32),
                        pltpu.VMEM((t, pair_v), F32)],
        compiler_params=_params(("parallel", "parallel", "arbitrary")),
        name="mla_attn",
    )(q, kt, va, vb)
    return _out_proj(x3.reshape(b * s, d), o.reshape(b * s, hh * MLA_V), w_out).reshape(b, s, d)


def kernel(x, rel_bias_table, ffn_norm, ffn_w_in, ffn_w_out, mixer_norm, conv_w_in, conv_b_in, conv_w_dw, conv_b_dw, conv_ln_g, conv_ln_b, conv_w_out, conv_b_out, diff_w_in, diff_q_norm, diff_k_norm, diff_lambda, diff_sub_norm, diff_w_out, sb_w_in, sb_w_out, mla_w_a, mla_q_a_norm, mla_kv_a_norm, mla_w_uq, mla_w_ukv, mla_q_norm, mla_k_norm, mla_w_out):
    b, s, d = x.shape
    assert d == D_MODEL and s % TOK_TILE == 0
    depth = ffn_norm.shape[0]
    ffn = lambda xx, i, k: _ffn(xx.reshape(b * s, d), ffn_norm[i, k], ffn_w_in[i, k],
                                ffn_w_out[i, k]).reshape(b, s, d)
    for i in range(depth):
        mixer, j = i % N_MIXERS, i // N_MIXERS
        x = ffn(x, i, 0)
        g = mixer_norm[i]
        if mixer == 0:
            x = _conv_mixer(x, g, conv_w_in[j], conv_b_in[j], conv_w_dw[j], conv_b_dw[j],
                            conv_ln_g[j], conv_ln_b[j], conv_w_out[j], conv_b_out[j])
        elif mixer == 1:
            x = _diff_mixer(x, i, g, diff_w_in[j], diff_q_norm[j], diff_k_norm[j], diff_lambda[j],
                            diff_sub_norm[j], diff_w_out[j], rel_bias_table)
        elif mixer == 2:
            x = _sb_mixer(x, g, sb_w_in[j], sb_w_out[j])
        else:
            x = _mla_mixer(x, g, mla_w_a[j], mla_q_a_norm[j], mla_kv_a_norm[j], mla_w_uq[j],
                           mla_w_ukv[j], mla_q_norm[j], mla_k_norm[j], mla_w_out[j])
        x = ffn(x, i, 1)
    return x
```

```python
import math

import jax
import jax.numpy as jnp
from jax import lax
from jax.experimental import pallas as pl
from jax.experimental.pallas import tpu as pltpu

F32 = jnp.float32
BF16 = jnp.bfloat16

D_MODEL = 1024
DEPTH = 4
CHUNK = 64
N_MIXERS = 4
D_FF = 2816
RMS_EPS = 1e-6
LN_EPS = 1e-5
MASK_VALUE = -1e30
CONV_WIDTH = 31
DIFF_HEADS = 8
DIFF_QK_DIM = 64
DIFF_V_DIM = 128
LAMBDA_INIT_BASE = 0.8
LAMBDA_INIT_SCALE = 0.6
LAMBDA_INIT_DECAY = 0.3
REL_BUCKETS = 32
REL_MAX_DIST = 128
SB_HEADS = 16
SB_HEAD_DIM = 64
MLA_HEADS = 16
MLA_Q_LORA = 384
MLA_KV_LORA = 256
MLA_NOPE = 64
MLA_ROPE = 32
MLA_V = 64
ROPE_BASE = 10000.0
LOG2E = math.log2(math.e)

LANES = 128
SUBLANES = 8
TOK_TILE = 512
FF_CHUNK = 256
SB_KEY_BLOCK = 128
CONV_HALO = 32
CONV_ROWS = 32
VMEM_LIMIT = 56 * 1024 * 1024
F32_EXP_ZERO = -104.0


def _params(sem):
    return pltpu.CompilerParams(dimension_semantics=sem, vmem_limit_bytes=VMEM_LIMIT)


def _const_spec(shape):
    nd = len(shape)
    return pl.BlockSpec(shape, lambda *_: (0,) * nd, pipeline_mode=pl.Buffered(1))


def _rms_bf16(x, g):
    ms = jnp.mean(x * x, axis=-1, keepdims=True)
    return (x * lax.rsqrt(ms + RMS_EPS) * g).astype(BF16)


def _split_dot(x, w):
    hi = x.astype(BF16)
    lo = (x - hi.astype(F32)).astype(BF16)
    return (jnp.dot(hi, w, preferred_element_type=F32)
            + jnp.dot(lo, w, preferred_element_type=F32))


def _dot_nt(a, b):
    return lax.dot_general(a, b, (((1,), (1,)), ((), ())), preferred_element_type=F32)


def _rows_rms(y, g_col):
    ms = jnp.mean(y * y, axis=0, keepdims=True)
    return y * lax.rsqrt(ms + RMS_EPS) * g_col


def _ffn_kernel(x_ref, g_ref, win_ref, wout_ref, o_ref, gate_ref):
    x = x_ref[...]
    h = _rms_bf16(x, g_ref[...])
    for c in range(D_FF // FF_CHUNK):
        lo, hi = c * FF_CHUNK, (c + 1) * FF_CHUNK
        a = jnp.dot(h, win_ref[:, lo:hi], preferred_element_type=F32)
        u = jnp.dot(h, win_ref[:, D_FF + lo:D_FF + hi], preferred_element_type=F32)
        gate_ref[:, lo:hi] = (a * jax.nn.sigmoid(a) * u).astype(BF16)
    y = jnp.dot(gate_ref[...], wout_ref[...], preferred_element_type=F32)
    o_ref[...] = x + 0.5 * y


def _ffn(x2, g, w_in, w_out):
    n, d = x2.shape
    tm = TOK_TILE
    return pl.pallas_call(
        _ffn_kernel,
        grid=(n // tm,),
        in_specs=[pl.BlockSpec((tm, d), lambda i: (i, 0)),
                  _const_spec((1, d)),
                  _const_spec((d, 2 * D_FF)),
                  _const_spec((D_FF, d))],
        out_specs=pl.BlockSpec((tm, d), lambda i: (i, 0)),
        out_shape=jax.ShapeDtypeStruct((n, d), F32),
        scratch_shapes=[pltpu.VMEM((tm, D_FF), BF16)],
        compiler_params=_params(("parallel",)),
        name="ffn",
    )(x2, g.reshape(1, d), w_in.astype(BF16), w_out.astype(BF16))


def _out_proj_kernel(x_ref, o_ref, w_ref, y_ref):
    y_ref[...] = x_ref[...] + jnp.dot(o_ref[...], w_ref[...], preferred_element_type=F32)


def _out_proj(x2, o2, w):
    n, d = x2.shape
    kd = o2.shape[1]
    tm = TOK_TILE
    return pl.pallas_call(
        _out_proj_kernel,
        grid=(n // tm,),
        in_specs=[pl.BlockSpec((tm, d), lambda i: (i, 0)),
                  pl.BlockSpec((tm, kd), lambda i: (i, 0)),
                  _const_spec((kd, d))],
        out_specs=pl.BlockSpec((tm, d), lambda i: (i, 0)),
        out_shape=jax.ShapeDtypeStruct((n, d), F32),
        compiler_params=_params(("parallel",)),
        name="out_proj",
    )(x2, o2, w.astype(BF16))


def _conv_in_kernel(x_ref, g_ref, w_ref, b_ref, u_ref):
    d = D_MODEL
    h = _rms_bf16(x_ref[...], g_ref[...])
    y = jnp.dot(h, w_ref[...], preferred_element_type=F32) + b_ref[...]
    u_ref[...] = y[:, :d] * jax.nn.sigmoid(y[:, d:])


def _conv_out_kernel(x_ref, ucur_ref, uprev_ref, wdw_ref, bdw_ref, lng_ref, lnb_ref,
                     wout_ref, bout_ref, y_ref, ext_ref, conv_ref):
    ts = ucur_ref.shape[1]
    i = pl.program_id(1)
    ext_ref[0:CONV_HALO, :] = jnp.where(i > 0, uprev_ref[0], 0.0)
    ext_ref[CONV_HALO:, :] = ucur_ref[0]
    off = CONV_HALO - (CONV_WIDTH - 1)
    for r in range(ts // CONV_ROWS):
        r0 = r * CONV_ROWS
        acc = ext_ref[r0 + off:r0 + off + CONV_ROWS, :] * wdw_ref[0:1, :]
        for k in range(1, CONV_WIDTH):
            acc = acc + ext_ref[r0 + off + k:r0 + off + k + CONV_ROWS, :] * wdw_ref[k:k + 1, :]
        conv_ref[r0:r0 + CONV_ROWS, :] = acc + bdw_ref[...]
    c = conv_ref[...]
    mu = jnp.mean(c, axis=-1, keepdims=True)
    cc = c - mu
    var = jnp.mean(cc * cc, axis=-1, keepdims=True)
    ln = cc * lax.rsqrt(var + LN_EPS) * lng_ref[...] + lnb_ref[...]
    act = (ln * jax.nn.sigmoid(ln)).astype(BF16)
    y = jnp.dot(act, wout_ref[...], preferred_element_type=F32) + bout_ref[...]
    y_ref[0] = x_ref[0] + y


def _conv_mixer(x3, g, w_in, b_in, w_dw, b_dw, ln_g, ln_b, w_out, b_out):
    b, s, d = x3.shape
    n = b * s
    tm = TOK_TILE
    u = pl.pallas_call(
        _conv_in_kernel,
        grid=(n // tm,),
        in_specs=[pl.BlockSpec((tm, d), lambda i: (i, 0)),
                  _const_spec((1, d)),
                  _const_spec((d, 2 * d)),
                  _const_spec((1, 2 * d))],
        out_specs=pl.BlockSpec((tm, d), lambda i: (i, 0)),
        out_shape=jax.ShapeDtypeStruct((n, d), F32),
        compiler_params=_params(("parallel",)),
        name="conv_in",
    )(x3.reshape(n, d), g.reshape(1, d), w_in.astype(BF16), b_in.reshape(1, 2 * d))
    u3 = u.reshape(b, s, d)
    halo_per_tile = tm // CONV_HALO
    row = lambda v: v.reshape(1, d)
    return pl.pallas_call(
        _conv_out_kernel,
        grid=(b, s // tm),
        in_specs=[pl.BlockSpec((1, tm, d), lambda bi, i: (bi, i, 0)),
                  pl.BlockSpec((1, tm, d), lambda bi, i: (bi, i, 0)),
                  pl.BlockSpec((1, CONV_HALO, d),
                               lambda bi, i: (bi, jnp.maximum(i * halo_per_tile - 1, 0), 0)),
                  _const_spec((CONV_WIDTH, d)),
                  _const_spec((1, d)), _const_spec((1, d)), _const_spec((1, d)),
                  _const_spec((d, d)),
                  _const_spec((1, d))],
        out_specs=pl.BlockSpec((1, tm, d), lambda bi, i: (bi, i, 0)),
        out_shape=jax.ShapeDtypeStruct((b, s, d), F32),
        scratch_shapes=[pltpu.VMEM((tm + CONV_HALO, d), F32), pltpu.VMEM((tm, d), F32)],
        compiler_params=_params(("parallel", "parallel")),
        name="conv_out",
    )(x3, u3, u3, w_dw, row(b_dw), row(ln_g), row(ln_b), w_out.astype(BF16), row(b_out))


def _softmax_step_t(s, vt, m_ref, l_ref, acc_ref):
    m_prev = m_ref[...]
    m_new = jnp.maximum(m_prev, jnp.max(s, axis=0, keepdims=True))
    alpha = jnp.exp2(m_prev - m_new)
    p = jnp.exp2(s - m_new)
    l_ref[...] = alpha * l_ref[...] + jnp.sum(p, axis=0, keepdims=True)
    acc_ref[...] = alpha * acc_ref[...] + jnp.dot(vt, p.astype(BF16), preferred_element_type=F32)
    m_ref[...] = m_new


def _group_mean_matrix(groups, width):
    idx = []
    for gi, size in enumerate(groups):
        idx += [gi] * size
    reps = width // len(idx)
    gid = jnp.asarray([r * len(groups) + g for r in range(reps) for g in idx], jnp.int32)
    sizes = jnp.asarray([float(groups[g]) for _ in range(reps) for g in idx], F32)
    same = gid[:, None] == gid[None, :]
    return jnp.where(same, 1.0 / sizes[None, :], 0.0).astype(BF16)


def _diff_proj_kernel(x_ref, g_ref, wqt_ref, wk_ref, wvt_ref, gq_ref, gk_ref, gmat_ref,
                      qt_ref, k_ref, vt_ref):
    h = _rms_bf16(x_ref[0], g_ref[...])
    yqt = _dot_nt(wqt_ref[...], h)
    for r in range(yqt.shape[0] // DIFF_QK_DIM):
        sl = slice(r * DIFF_QK_DIM, (r + 1) * DIFF_QK_DIM)
        qt_ref[0, 0, sl, :] = _rows_rms(yqt[sl, :], gq_ref[sl, :]).astype(BF16)
    yk = jnp.dot(h, wk_ref[...], preferred_element_type=F32)
    width = gmat_ref.shape[0]
    for c in range(yk.shape[1] // width):
        sl = slice(c * width, (c + 1) * width)
        y = yk[:, sl]
        ms = _split_dot(y * y, gmat_ref[...])
        k_ref[0, :, sl] = (y * lax.rsqrt(ms + RMS_EPS) * gk_ref[:, sl]).astype(BF16)
    vt_ref[0, 0] = _dot_nt(wvt_ref[...], h).astype(BF16)


def _diff_attn_kernel(tab_ref, scal_ref, qt_ref, k_ref, vt_ref, bucket_ref, subg_ref, o_ref,
                      bias_ref, m_ref, l_ref, acc_ref):
    t = qt_ref.shape[3]
    hd = pl.program_id(1)
    i = pl.program_id(2)

    @pl.when(i == 0)
    def _build_bias():
        far = tab_ref[REL_BUCKETS // 2 - 1, hd]
        keys = lax.broadcasted_iota(jnp.int32, (t, t), 0)
        queries = lax.broadcasted_iota(jnp.int32, (t, t), 1)
        visible = (keys // CHUNK) <= (queries // CHUNK)
        for tile in range(2):
            bk = bucket_ref[tile]
            bias = jnp.zeros((t, t), F32)
            for b in range(REL_BUCKETS):
                bias = jnp.where(bk == b, (tab_ref[b, hd] - far) * LOG2E, bias)
            if tile == 1:
                bias = jnp.where(visible, bias, MASK_VALUE)
            bias_ref[tile] = bias

    m_ref[...] = jnp.full(m_ref.shape, -jnp.inf, F32)
    l_ref[...] = jnp.zeros(l_ref.shape, F32)
    acc_ref[...] = jnp.zeros(acc_ref.shape, F32)

    qt = qt_ref[0, 0]
    feat = lax.broadcasted_iota(jnp.int32, qt.shape, 0)
    qmaps = [jnp.where((feat // DIFF_QK_DIM) == mp, qt, jnp.zeros_like(qt)) for mp in range(2)]

    def step(j, bias_tile):
        k = k_ref[0, pl.ds(pl.multiple_of(j * t, t), t), :]
        vt = vt_ref[0, j]
        ss = [jnp.dot(k, qmaps[mp], preferred_element_type=F32) for mp in range(2)]
        for mp in range(2):
            s = ss[mp]
            if bias_tile is not None:
                s = s + bias_ref[bias_tile]
            _softmax_step_t(s, vt, m_ref.at[mp], l_ref.at[mp], acc_ref.at[mp])

    def far_body(j, carry):
        step(j, None)
        return carry

    lax.fori_loop(0, i - 1, far_body, 0)

    @pl.when(i >= 1)
    def _prev():
        step(i - 1, 0)

    step(i, 1)

    lam = scal_ref[0]
    attn = acc_ref[0] / l_ref[0] - lam * (acc_ref[1] / l_ref[1])
    o_ref[0] = _rows_rms(attn, subg_ref[...]).T.astype(BF16)


def _t5_bucket(rel):
    nb = REL_BUCKETS // 2
    bucket = jnp.where(rel > 0, nb, 0)
    n = jnp.abs(rel)
    max_exact = nb // 2
    n_f = jnp.maximum(n, 1).astype(jnp.float32)
    large = max_exact + (jnp.log(n_f / max_exact) / math.log(REL_MAX_DIST / max_exact)
                         * (nb - max_exact)).astype(jnp.int32)
    large = jnp.minimum(large, nb - 1)
    return bucket + jnp.where(n < max_exact, n, large)


def _diff_mixer(x3, layer_idx, g, w_in, q_norm_g, k_norm_g, lam, sub_norm_g, w_out, rel_table):
    b, s, d = x3.shape
    t = TOK_TILE
    nb = s // t
    hh, dq, dv = DIFF_HEADS, DIFF_QK_DIM, DIFF_V_DIM
    w3 = w_in.reshape(d, hh, 4 * dq + dv)
    wqt = w3[:, :, :2 * dq].reshape(d, hh * 2 * dq).T.astype(BF16)
    wk = w3[:, :, 2 * dq:4 * dq].reshape(d, hh * 2 * dq).astype(BF16)
    wvt = w3[:, :, 4 * dq:].reshape(d, hh * dv).T.astype(BF16)
    scale = dq ** -0.5
    gq = (jnp.tile(q_norm_g, 2 * hh) * (scale * LOG2E)).reshape(hh * 2 * dq, 1)
    gk = jnp.tile(k_norm_g, 2 * hh).reshape(1, hh * 2 * dq)
    gmat = _group_mean_matrix([dq], 2 * LANES)
    feat_major = lambda w: pl.BlockSpec((1, 1, w, t), lambda bi, i: (bi, i, 0, 0))
    qt, k, vt = pl.pallas_call(
        _diff_proj_kernel,
        grid=(b, nb),
        in_specs=[pl.BlockSpec((1, t, d), lambda bi, i: (bi, i, 0)),
                  _const_spec((1, d)),
                  _const_spec((hh * 2 * dq, d)),
                  _const_spec((d, hh * 2 * dq)),
                  _const_spec((hh * dv, d)),
                  _const_spec((hh * 2 * dq, 1)),
                  _const_spec((1, hh * 2 * dq)),
                  _const_spec((2 * LANES, 2 * LANES))],
        out_specs=[feat_major(hh * 2 * dq),
                   pl.BlockSpec((1, t, hh * 2 * dq), lambda bi, i: (bi, i, 0)),
                   feat_major(hh * dv)],
        out_shape=[jax.ShapeDtypeStruct((b, nb, hh * 2 * dq, t), BF16),
                   jax.ShapeDtypeStruct((b, s, hh * 2 * dq), BF16),
                   jax.ShapeDtypeStruct((b, nb, hh * dv, t), BF16)],
        compiler_params=_params(("parallel", "parallel")),
        name="diff_proj",
    )(x3, g.reshape(1, d), wqt, wk, wvt, gq, gk, gmat)

    lam_init = LAMBDA_INIT_BASE - LAMBDA_INIT_SCALE * math.exp(-LAMBDA_INIT_DECAY * layer_idx)
    lam_full = (jnp.exp(jnp.sum(lam[0] * lam[1]).astype(F32))
                - jnp.exp(jnp.sum(lam[2] * lam[3]).astype(F32)) + lam_init)
    scal = jnp.reshape(lam_full, (1,)).astype(F32)
    r = jnp.arange(t)
    rel_diag = r[:, None] - r[None, :]
    buckets = jnp.stack([_t5_bucket(rel_diag - t), _t5_bucket(rel_diag)]).astype(jnp.int32)
    subg = (sub_norm_g * (1.0 - lam_init)).reshape(dv, 1)
    smem = pl.BlockSpec(memory_space=pltpu.SMEM)
    o = pl.pallas_call(
        _diff_attn_kernel,
        grid=(b, hh, nb),
        in_specs=[smem, smem,
                  pl.BlockSpec((1, 1, 2 * dq, t), lambda bi, h, i: (bi, i, h, 0)),
                  pl.BlockSpec((1, s, 2 * dq), lambda bi, h, i: (bi, 0, h)),
                  pl.BlockSpec((1, nb, dv, t), lambda bi, h, i: (bi, 0, h, 0)),
                  _const_spec((2, t, t)),
                  _const_spec((dv, 1))],
        out_specs=pl.BlockSpec((1, t, dv), lambda bi, h, i: (bi, i, h)),
        out_shape=jax.ShapeDtypeStruct((b, s, hh * dv), BF16),
        scratch_shapes=[pltpu.VMEM((2, t, t), F32),
                        pltpu.VMEM((2, 1, t), F32),
                        pltpu.VMEM((2, 1, t), F32),
                        pltpu.VMEM((2, dv, t), F32)],
        compiler_params=_params(("parallel", "parallel", "arbitrary")),
        name="diff_attn",
    )(rel_table.astype(F32), scal, qt, k, vt, buckets, subg)
    return _out_proj(x3.reshape(b * s, d), o.reshape(b * s, hh * dv), w_out).reshape(b, s, d)


def _sb_proj_kernel(x_ref, g_ref, wqt_ref, wk_ref, wvt_ref, qt_ref, k_ref, vt_ref):
    h = _rms_bf16(x_ref[0], g_ref[...])
    qt_ref[0, 0] = (_dot_nt(wqt_ref[...], h) * (SB_HEAD_DIM ** -0.5)).astype(BF16)
    k_ref[0] = jnp.dot(h, wk_ref[...], preferred_element_type=F32).astype(BF16)
    vt_ref[0, 0] = _dot_nt(wvt_ref[...], h).astype(BF16)


def _sb_attn_kernel(qt_ref, k_ref, vt_ref, tri_ref, o_ref, run_ref, acc_ref):
    t = qt_ref.shape[3]
    kb = SB_KEY_BLOCK
    dh = SB_HEAD_DIM
    i = pl.program_id(2)
    sub = t // kb
    qt = qt_ref[0, 0]
    feat = lax.broadcasted_iota(jnp.int32, qt.shape, 0)
    qh = [jnp.where((feat // dh) == a, qt, jnp.zeros_like(qt)) for a in range(2)]
    run_ref[...] = jnp.zeros(run_ref.shape, F32)
    acc_ref[...] = jnp.zeros(acc_ref.shape, F32)

    def sweep(j, causal):
        k = k_ref[0, pl.ds(pl.multiple_of(j * t, t), t), :]
        vt = vt_ref[0, j]
        zs = [jnp.dot(k, qh[a], preferred_element_type=F32) for a in range(2)]
        for a in range(2):
            z = zs[a]
            soft = jnp.log(1.0 + jnp.exp(-jnp.abs(z)))
            log_beta = jnp.minimum(z, 0.0) - soft
            log_keep = log_beta - z
            if causal is not None:
                log_keep = jnp.where(causal, log_keep, 0.0)
            hi = log_keep.astype(BF16)
            lo = (log_keep - hi.astype(F32)).astype(BF16)
            run = run_ref[a]
            ws = [None] * sub
            for blk in range(sub - 1, -1, -1):
                rows = slice(blk * kb, (blk + 1) * kb)
                sums = jnp.dot(tri_ref[...], jnp.concatenate([hi[rows], lo[rows]], axis=0),
                               preferred_element_type=F32)
                ws[blk] = jnp.exp(log_beta[rows] + sums[:kb] + run[0:1])
                run = run + sums[kb:]
            w = jnp.concatenate(ws, axis=0)
            if causal is not None:
                w = jnp.where(causal, w, 0.0)
            acc_ref[a] += jnp.dot(vt[a * dh:(a + 1) * dh, :], w.astype(BF16),
                                  preferred_element_type=F32)
            run_ref[a] = run

    keys = lax.broadcasted_iota(jnp.int32, (t, t), 0)
    queries = lax.broadcasted_iota(jnp.int32, (t, t), 1)
    sweep(i, keys < queries)

    def cond(c):
        j, top = c
        return jnp.logical_and(j >= 0, top > F32_EXP_ZERO)

    def body(c):
        j, _ = c
        sweep(j, None)
        return j - 1, jnp.max(run_ref[...])

    lax.while_loop(cond, body, (i - 1, jnp.max(run_ref[...])))
    o_ref[0] = jnp.concatenate([acc_ref[0], acc_ref[1]], axis=0).T.astype(BF16)


def _sb_mixer(x3, g, w_in, w_out):
    b, s, d = x3.shape
    t = TOK_TILE
    kb = SB_KEY_BLOCK
    nb = s // t
    hh, dh = SB_HEADS, SB_HEAD_DIM
    w3 = w_in.reshape(d, hh, 3 * dh)
    wqt = w3[:, :, :dh].reshape(d, hh * dh).T.astype(BF16)
    wk = w3[:, :, dh:2 * dh].reshape(d, hh * dh).astype(BF16)
    wvt = w3[:, :, 2 * dh:].reshape(d, hh * dh).T.astype(BF16)
    qt, k, vt = pl.pallas_call(
        _sb_proj_kernel,
        grid=(b, nb),
        in_specs=[pl.BlockSpec((1, t, d), lambda bi, i: (bi, i, 0)),
                  _const_spec((1, d)),
                  _const_spec((hh * dh, d)),
                  _const_spec((d, hh * dh)),
                  _const_spec((hh * dh, d))],
        out_specs=[pl.BlockSpec((1, 1, hh * dh, t), lambda bi, i: (bi, i, 0, 0)),
                   pl.BlockSpec((1, t, hh * dh), lambda bi, i: (bi, i, 0)),
                   pl.BlockSpec((1, 1, hh * dh, t), lambda bi, i: (bi, i, 0, 0))],
        out_shape=[jax.ShapeDtypeStruct((b, nb, hh * dh, t), BF16),
                   jax.ShapeDtypeStruct((b, s, hh * dh), BF16),
                   jax.ShapeDtypeStruct((b, nb, hh * dh, t), BF16)],
        compiler_params=_params(("parallel", "parallel")),
        name="sb_proj",
    )(x3, g.reshape(1, d), wqt, wk, wvt)

    rj = jnp.arange(kb + SUBLANES)
    sk = jnp.arange(2 * kb) % kb
    tri = jnp.where((rj[:, None] >= kb) | (sk[None, :] > rj[:, None]), 1.0, 0.0).astype(BF16)
    pair = 2 * dh
    o = pl.pallas_call(
        _sb_attn_kernel,
        grid=(b, hh // 2, nb),
        in_specs=[pl.BlockSpec((1, 1, pair, t), lambda bi, h, i: (bi, i, h, 0)),
                  pl.BlockSpec((1, s, pair), lambda bi, h, i: (bi, 0, h)),
                  pl.BlockSpec((1, nb, pair, t), lambda bi, h, i: (bi, 0, h, 0)),
                  _const_spec((kb + SUBLANES, 2 * kb))],
        out_specs=pl.BlockSpec((1, t, pair), lambda bi, h, i: (bi, i, h)),
        out_shape=jax.ShapeDtypeStruct((b, s, hh * dh), BF16),
        scratch_shapes=[pltpu.VMEM((2, SUBLANES, t), F32), pltpu.VMEM((2, dh, t), F32)],
        compiler_params=_params(("parallel", "parallel", "arbitrary")),
        name="sb_attn",
    )(qt, k, vt, tri)
    return _out_proj(x3.reshape(b * s, d), o.reshape(b * s, hh * dh), w_out).reshape(b, s, d)


MLA_QK_PAD = MLA_NOPE + 2 * MLA_ROPE


def _mla_proj_kernel(x_ref, g_ref, waq_ref, wakv_ref, war_ref, gqa_ref, gkva_ref, wuqt_ref,
                     wuk_ref, wuvt_ref, gmat_ref, gq_ref, cst_ref, gkn_ref, gkr_ref, csk_ref,
                     qt_ref, k_ref, vt_ref):
    nope, rope, qp = MLA_NOPE, MLA_ROPE, MLA_QK_PAD
    h = _rms_bf16(x_ref[0], g_ref[...])
    cq = _rms_bf16(jnp.dot(h, waq_ref[...], preferred_element_type=F32), gqa_ref[...])
    ckv = _rms_bf16(jnp.dot(h, wakv_ref[...], preferred_element_type=F32), gkva_ref[...])
    yqt = _dot_nt(wuqt_ref[...], cq)
    cs = cst_ref[...]
    for hd in range(MLA_HEADS):
        base = hd * qp
        y = yqt[base:base + qp, :]
        gcol = gq_ref[base:base + qp, :]
        qt_ref[0, 0, base:base + nope, :] = _rows_rms(y[:nope], gcol[:nope]).astype(BF16)
        msr = jnp.mean(y[nope:nope + rope] * y[nope:nope + rope], axis=0, keepdims=True)
        rr = y[nope:] * lax.rsqrt(msr + RMS_EPS) * gcol[nope:] * cs
        qf = (rr[:rope] + rr[rope:]).astype(BF16)
        qt_ref[0, 0, base + nope:base + nope + rope, :] = qf
        qt_ref[0, 0, base + nope + rope:base + qp, :] = qf
    width = gmat_ref.shape[0]
    gm = gmat_ref[...]
    sh = jnp.dot(h, war_ref[...], preferred_element_type=F32)
    ms = _split_dot(sh * sh, gm[:qp, :qp])
    shn = sh * lax.rsqrt(ms + RMS_EPS) * gkr_ref[...] * csk_ref[...]
    shared = jnp.concatenate([shn] * (width // qp), axis=1)
    ykn = jnp.dot(ckv, wuk_ref[...], preferred_element_type=F32)
    for c in range(ykn.shape[1] // width):
        sl = slice(c * width, (c + 1) * width)
        y = ykn[:, sl]
        ms = _split_dot(y * y, gm)
        k_ref[0, :, sl] = (y * lax.rsqrt(ms + RMS_EPS) * gkn_ref[:, sl] + shared).astype(BF16)
    vt_ref[0, 0] = _dot_nt(wuvt_ref[...], ckv).astype(BF16)


def _mla_attn_kernel(qt_ref, k_ref, vt_ref, o_ref, m_ref, l_ref, acc_ref):
    t = qt_ref.shape[3]
    qp, dv = MLA_QK_PAD, MLA_V
    i = pl.program_id(2)
    m_ref[...] = jnp.full(m_ref.shape, -jnp.inf, F32)
    l_ref[...] = jnp.zeros(l_ref.shape, F32)
    acc_ref[...] = jnp.zeros(acc_ref.shape, F32)

    def step(j, visible):
        k = k_ref[0, pl.ds(pl.multiple_of(j * t, t), t), :]
        vt = vt_ref[0, j]
        ss = [jnp.dot(k[:, a * qp:(a + 1) * qp], qt_ref[0, 0, a * qp:(a + 1) * qp, :],
                      preferred_element_type=F32) for a in range(2)]
        for a in range(2):
            s = ss[a]
            if visible is not None:
                s = jnp.where(visible, s, MASK_VALUE)
            _softmax_step_t(s, vt[a * dv:(a + 1) * dv, :], m_ref.at[a], l_ref.at[a], acc_ref.at[a])

    def body(j, carry):
        step(j, None)
        return carry

    lax.fori_loop(0, i, body, 0)
    keys = lax.broadcasted_iota(jnp.int32, (t, t), 0)
    queries = lax.broadcasted_iota(jnp.int32, (t, t), 1)
    step(i, (keys // CHUNK) <= (queries // CHUNK))
    o = jnp.concatenate([acc_ref[0] / l_ref[0], acc_ref[1] / l_ref[1]], axis=0)
    o_ref[0] = o.T.astype(BF16)


def _rotate_half_cols(w):
    half = MLA_ROPE // 2
    return jnp.concatenate([-w[..., half:], w[..., :half]], axis=-1)


def _mla_mixer(x3, g, w_a, q_a_norm, kv_a_norm, w_uq, w_ukv, q_norm_g, k_norm_g, w_out):
    b, s, d = x3.shape
    t = TOK_TILE
    nb = s // t
    hh = MLA_HEADS
    nope, rope, qp, dv = MLA_NOPE, MLA_ROPE, MLA_QK_PAD, MLA_V
    half = rope // 2
    wa_q = w_a[:, :MLA_Q_LORA].astype(BF16)
    wa_kv = w_a[:, MLA_Q_LORA:MLA_Q_LORA + MLA_KV_LORA].astype(BF16)
    wa_r = w_a[:, MLA_Q_LORA + MLA_KV_LORA:]
    war = jnp.concatenate([jnp.zeros((d, nope), F32), wa_r, _rotate_half_cols(wa_r)], axis=1).astype(BF16)
    wq3 = w_uq.reshape(MLA_Q_LORA, hh, nope + rope)
    wuqt = jnp.concatenate([wq3, _rotate_half_cols(wq3[:, :, nope:])], axis=2
                           ).reshape(MLA_Q_LORA, hh * qp).T.astype(BF16)
    wkv3 = w_ukv.reshape(MLA_KV_LORA, hh, nope + dv)
    wuk = jnp.concatenate([wkv3[:, :, :nope], jnp.zeros((MLA_KV_LORA, hh, 2 * rope), F32)], axis=2
                          ).reshape(MLA_KV_LORA, hh * qp).astype(BF16)
    wuvt = wkv3[:, :, nope:].reshape(MLA_KV_LORA, hh * dv).T.astype(BF16)
    swap = lambda v: jnp.concatenate([v[half:], v[:half]])
    scale = (nope + rope) ** -0.5
    gq_head = jnp.concatenate([q_norm_g, swap(q_norm_g[nope:])]) * (scale * LOG2E)
    gq = jnp.tile(gq_head, hh).reshape(hh * qp, 1)
    gkn = jnp.tile(jnp.concatenate([k_norm_g[:nope], jnp.zeros((2 * rope,), F32)]), hh).reshape(1, hh * qp)
    gkr = jnp.concatenate([jnp.zeros((nope,), F32), k_norm_g[nope:], swap(k_norm_g[nope:])]).reshape(1, qp)
    inv = ROPE_BASE ** (-jnp.arange(0, rope, 2, dtype=F32) / rope)
    ang = jnp.arange(s, dtype=F32)[:, None] * inv[None, :]
    cos, sin = jnp.cos(ang), jnp.sin(ang)
    cs = jnp.concatenate([cos, cos, sin, sin], axis=1)
    csk = jnp.concatenate([jnp.zeros((s, nope), F32), cs], axis=1)
    gmat = _group_mean_matrix([nope, rope, rope], 2 * qp)
    feat_major = lambda w: pl.BlockSpec((1, 1, w, t), lambda bi, i: (bi, i, 0, 0))
    qt, k, vt = pl.pallas_call(
        _mla_proj_kernel,
        grid=(b, nb),
        in_specs=[pl.BlockSpec((1, t, d), lambda bi, i: (bi, i, 0)),
                  _const_spec((1, d)),
                  _const_spec((d, MLA_Q_LORA)),
                  _const_spec((d, MLA_KV_LORA)),
                  _const_spec((d, qp)),
                  _const_spec((1, MLA_Q_LORA)),
                  _const_spec((1, MLA_KV_LORA)),
                  _const_spec((hh * qp, MLA_Q_LORA)),
                  _const_spec((MLA_KV_LORA, hh * qp)),
                  _const_spec((hh * dv, MLA_KV_LORA)),
                  _const_spec((2 * qp, 2 * qp)),
                  _const_spec((hh * qp, 1)),
                  pl.BlockSpec((2 * rope, t), lambda bi, i: (0, i)),
                  _const_spec((1, hh * qp)),
                  _const_spec((1, qp)),
                  pl.BlockSpec((t, qp), lambda bi, i: (i, 0))],
        out_specs=[feat_major(hh * qp),
                   pl.BlockSpec((1, t, hh * qp), lambda bi, i: (bi, i, 0)),
                   feat_major(hh * dv)],
        out_shape=[jax.ShapeDtypeStruct((b, nb, hh * qp, t), BF16),
                   jax.ShapeDtypeStruct((b, s, hh * qp), BF16),
                   jax.ShapeDtypeStruct((b, nb, hh * dv, t), BF16)],
        compiler_params=_params(("parallel", "parallel")),
        name="mla_proj",
    )(x3, g.reshape(1, d), wa_q, wa_kv, war, q_a_norm.reshape(1, -1), kv_a_norm.reshape(1, -1),
      wuqt, wuk, wuvt, gmat, gq, cs.T, gkn, gkr, csk)

    o = pl.pallas_call(
        _mla_attn_kernel,
        grid=(b, hh // 2, nb),
        in_specs=[pl.BlockSpec((1, 1, 2 * qp, t), lambda bi, h, i: (bi, i, h, 0)),
                  pl.BlockSpec((1, s, 2 * qp), lambda bi, h, i: (bi, 0, h)),
                  pl.BlockSpec((1, nb, 2 * dv, t), lambda bi, h, i: (bi, 0, h, 0))],
        out_specs=pl.BlockSpec((1, t, 2 * dv), lambda bi, h, i: (bi, i, h)),
        out_shape=jax.ShapeDtypeStruct((b, s, hh * dv), BF16),
        scratch_shapes=[pltpu.VMEM((2, 1, t), F32), pltpu.VMEM((2, 1, t), F32),
                        pltpu.VMEM((2, dv, t), F32)],
        compiler_params=_params(("parallel", "parallel", "arbitrary")),
        name="mla_attn",
    )(qt, k, vt)
    return _out_proj(x3.reshape(b * s, d), o.reshape(b * s, hh * dv), w_out).reshape(b, s, d)


def kernel(x, rel_bias_table, ffn_norm, ffn_w_in, ffn_w_out, mixer_norm, conv_w_in, conv_b_in, conv_w_dw, conv_b_dw, conv_ln_g, conv_ln_b, conv_w_out, conv_b_out, diff_w_in, diff_q_norm, diff_k_norm, diff_lambda, diff_sub_norm, diff_w_out, sb_w_in, sb_w_out, mla_w_a, mla_q_a_norm, mla_kv_a_norm, mla_w_uq, mla_w_ukv, mla_q_norm, mla_k_norm, mla_w_out):
    b, s, d = x.shape
    assert d == D_MODEL and s % TOK_TILE == 0
    depth = ffn_norm.shape[0]
    ffn = lambda xx, i, k: _ffn(xx.reshape(b * s, d), ffn_norm[i, k], ffn_w_in[i, k],
                                ffn_w_out[i, k]).reshape(b, s, d)
    for i in range(depth):
        mixer, j = i % N_MIXERS, i // N_MIXERS
        x = ffn(x, i, 0)
        g = mixer_norm[i]
        if mixer == 0:
            x = _conv_mixer(x, g, conv_w_in[j], conv_b_in[j], conv_w_dw[j], conv_b_dw[j],
                            conv_ln_g[j], conv_ln_b[j], conv_w_out[j], conv_b_out[j])
        elif mixer == 1:
            x = _diff_mixer(x, i, g, diff_w_in[j], diff_q_norm[j], diff_k_norm[j], diff_lambda[j],
                            diff_sub_norm[j], diff_w_out[j], rel_bias_table)
        elif mixer == 2:
            x = _sb_mixer(x, g, sb_w_in[j], sb_w_out[j])
        else:
            x = _mla_mixer(x, g, mla_w_a[j], mla_q_a_norm[j], mla_kv_a_norm[j], mla_w_uq[j],
                           mla_w_ukv[j], mla_q_norm[j], mla_k_norm[j], mla_w_out[j])
        x = ffn(x, i, 1)
    return x
```

```python
import math

import jax
import jax.numpy as jnp
from jax import lax
from jax.experimental import pallas as pl
from jax.experimental.pallas import tpu as pltpu

F32 = jnp.float32
BF16 = jnp.bfloat16

D_MODEL = 1024
DEPTH = 4
CHUNK = 64
N_MIXERS = 4
D_FF = 2816
RMS_EPS = 1e-6
LN_EPS = 1e-5
MASK_VALUE = -1e30
CONV_WIDTH = 31
DIFF_HEADS = 8
DIFF_QK_DIM = 64
DIFF_V_DIM = 128
LAMBDA_INIT_BASE = 0.8
LAMBDA_INIT_SCALE = 0.6
LAMBDA_INIT_DECAY = 0.3
REL_BUCKETS = 32
REL_MAX_DIST = 128
SB_HEADS = 16
SB_HEAD_DIM = 64
MLA_HEADS = 16
MLA_Q_LORA = 384
MLA_KV_LORA = 256
MLA_NOPE = 64
MLA_ROPE = 32
MLA_V = 64
ROPE_BASE = 10000.0
LOG2E = math.log2(math.e)

LANES = 128
SUBLANES = 8
TOK_TILE = 512
FF_CHUNK = 256
FAR_UNROLL = 4
SB_KEY_BLOCK = 128
SB_SWEEP_KEYS = 256
CONV_HALO = 32
CONV_ROWS = 32
VMEM_LIMIT = 56 * 1024 * 1024
F32_EXP_ZERO = -104.0


def _params(sem):
    return pltpu.CompilerParams(dimension_semantics=sem, vmem_limit_bytes=VMEM_LIMIT)


def _const_spec(shape):
    nd = len(shape)
    return pl.BlockSpec(shape, lambda *_: (0,) * nd, pipeline_mode=pl.Buffered(1))


def _rms_bf16(x, g):
    ms = jnp.mean(x * x, axis=-1, keepdims=True)
    return (x * lax.rsqrt(ms + RMS_EPS) * g).astype(BF16)


def _split_dot(x, w):
    hi = x.astype(BF16)
    lo = (x - hi.astype(F32)).astype(BF16)
    return (jnp.dot(hi, w, preferred_element_type=F32)
            + jnp.dot(lo, w, preferred_element_type=F32))


def _dot_nt(a, b):
    return lax.dot_general(a, b, (((1,), (1,)), ((), ())), preferred_element_type=F32)


def _rows_rms(y, g_col):
    ms = jnp.mean(y * y, axis=0, keepdims=True)
    return y * lax.rsqrt(ms + RMS_EPS) * g_col


def _ffn_kernel(x_ref, g_ref, win_ref, wout_ref, o_ref, gate_ref):
    x = x_ref[...]
    h = _rms_bf16(x, g_ref[...])
    for c in range(D_FF // FF_CHUNK):
        lo, hi = c * FF_CHUNK, (c + 1) * FF_CHUNK
        a = jnp.dot(h, win_ref[:, lo:hi], preferred_element_type=F32)
        u = jnp.dot(h, win_ref[:, D_FF + lo:D_FF + hi], preferred_element_type=F32)
        gate_ref[:, lo:hi] = (a * jax.nn.sigmoid(a) * u).astype(BF16)
    y = jnp.dot(gate_ref[...], wout_ref[...], preferred_element_type=F32)
    o_ref[...] = x + 0.5 * y


def _ffn(x2, g, w_in, w_out):
    n, d = x2.shape
    tm = TOK_TILE
    return pl.pallas_call(
        _ffn_kernel,
        grid=(n // tm,),
        in_specs=[pl.BlockSpec((tm, d), lambda i: (i, 0)),
                  _const_spec((1, d)),
                  _const_spec((d, 2 * D_FF)),
                  _const_spec((D_FF, d))],
        out_specs=pl.BlockSpec((tm, d), lambda i: (i, 0)),
        out_shape=jax.ShapeDtypeStruct((n, d), F32),
        scratch_shapes=[pltpu.VMEM((tm, D_FF), BF16)],
        compiler_params=_params(("parallel",)),
        name="ffn",
    )(x2, g.reshape(1, d), w_in.astype(BF16), w_out.astype(BF16))


def _out_proj_kernel(x_ref, o_ref, w_ref, y_ref):
    y_ref[...] = x_ref[...] + jnp.dot(o_ref[...], w_ref[...], preferred_element_type=F32)


def _out_proj(x2, o2, w):
    n, d = x2.shape
    kd = o2.shape[1]
    tm = TOK_TILE
    return pl.pallas_call(
        _out_proj_kernel,
        grid=(n // tm,),
        in_specs=[pl.BlockSpec((tm, d), lambda i: (i, 0)),
                  pl.BlockSpec((tm, kd), lambda i: (i, 0)),
                  _const_spec((kd, d))],
        out_specs=pl.BlockSpec((tm, d), lambda i: (i, 0)),
        out_shape=jax.ShapeDtypeStruct((n, d), F32),
        compiler_params=_params(("parallel",)),
        name="out_proj",
    )(x2, o2, w.astype(BF16))


def _conv_in_kernel(x_ref, g_ref, w_ref, b_ref, u_ref):
    d = D_MODEL
    h = _rms_bf16(x_ref[...], g_ref[...])
    y = jnp.dot(h, w_ref[...], preferred_element_type=F32) + b_ref[...]
    u_ref[...] = y[:, :d] * jax.nn.sigmoid(y[:, d:])


def _conv_out_kernel(x_ref, ucur_ref, uprev_ref, wdw_ref, bdw_ref, lng_ref, lnb_ref,
                     wout_ref, bout_ref, y_ref, ext_ref, conv_ref):
    ts = ucur_ref.shape[1]
    i = pl.program_id(1)
    ext_ref[0, 0:CONV_HALO, :] = jnp.where(i > 0, uprev_ref[0], 0.0)
    ext_ref[0, CONV_HALO:, :] = ucur_ref[0]
    n_shift = ts + CONV_HALO - SUBLANES
    for p in range(1, SUBLANES):
        ext_ref[p, 0:n_shift, :] = ext_ref[0, p:p + n_shift, :]
    off = CONV_HALO - (CONV_WIDTH - 1)
    for r in range(ts // CONV_ROWS):
        r0 = r * CONV_ROWS
        acc = None
        for k in range(CONV_WIDTH):
            p = (off + k) % SUBLANES
            a = r0 + off + k - p
            tap = ext_ref[p, a:a + CONV_ROWS, :] * wdw_ref[k:k + 1, :]
            acc = tap if acc is None else acc + tap
        conv_ref[r0:r0 + CONV_ROWS, :] = acc + bdw_ref[...]
    c = conv_ref[...]
    mu = jnp.mean(c, axis=-1, keepdims=True)
    cc = c - mu
    var = jnp.mean(cc * cc, axis=-1, keepdims=True)
    ln = cc * lax.rsqrt(var + LN_EPS) * lng_ref[...] + lnb_ref[...]
    act = (ln * jax.nn.sigmoid(ln)).astype(BF16)
    y = jnp.dot(act, wout_ref[...], preferred_element_type=F32) + bout_ref[...]
    y_ref[0] = x_ref[0] + y


def _conv_mixer(x3, g, w_in, b_in, w_dw, b_dw, ln_g, ln_b, w_out, b_out):
    b, s, d = x3.shape
    n = b * s
    tm = TOK_TILE
    u = pl.pallas_call(
        _conv_in_kernel,
        grid=(n // tm,),
        in_specs=[pl.BlockSpec((tm, d), lambda i: (i, 0)),
                  _const_spec((1, d)),
                  _const_spec((d, 2 * d)),
                  _const_spec((1, 2 * d))],
        out_specs=pl.BlockSpec((tm, d), lambda i: (i, 0)),
        out_shape=jax.ShapeDtypeStruct((n, d), F32),
        compiler_params=_params(("parallel",)),
        name="conv_in",
    )(x3.reshape(n, d), g.reshape(1, d), w_in.astype(BF16), b_in.reshape(1, 2 * d))
    u3 = u.reshape(b, s, d)
    halo_per_tile = tm // CONV_HALO
    row = lambda v: v.reshape(1, d)
    return pl.pallas_call(
        _conv_out_kernel,
        grid=(b, s // tm),
        in_specs=[pl.BlockSpec((1, tm, d), lambda bi, i: (bi, i, 0)),
                  pl.BlockSpec((1, tm, d), lambda bi, i: (bi, i, 0)),
                  pl.BlockSpec((1, CONV_HALO, d),
                               lambda bi, i: (bi, jnp.maximum(i * halo_per_tile - 1, 0), 0)),
                  _const_spec((CONV_WIDTH, d)),
                  _const_spec((1, d)), _const_spec((1, d)), _const_spec((1, d)),
                  _const_spec((d, d)),
                  _const_spec((1, d))],
        out_specs=pl.BlockSpec((1, tm, d), lambda bi, i: (bi, i, 0)),
        out_shape=jax.ShapeDtypeStruct((b, s, d), F32),
        scratch_shapes=[pltpu.VMEM((SUBLANES, tm + CONV_HALO, d), F32), pltpu.VMEM((tm, d), F32)],
        compiler_params=_params(("parallel", "parallel")),
        name="conv_out",
    )(x3, u3, u3, w_dw, row(b_dw), row(ln_g), row(ln_b), w_out.astype(BF16), row(b_out))


def _softmax_step_t(s, vt, m_ref, l_ref, acc_ref):
    m_prev = m_ref[...]
    m_new = jnp.maximum(m_prev, jnp.max(s, axis=0, keepdims=True))
    alpha = jnp.exp2(m_prev - m_new)
    p = jnp.exp2(s - m_new)
    l_ref[...] = alpha * l_ref[...] + jnp.sum(p, axis=0, keepdims=True)
    acc_ref[...] = alpha * acc_ref[...] + jnp.dot(vt, p.astype(BF16), preferred_element_type=F32)
    m_ref[...] = m_new


def _sweep_key_tiles(i, logits, absorb_far, absorb_prev, absorb_diag):
    n_far = jnp.maximum(i - 1, 0)
    n_groups = n_far // FAR_UNROLL

    def far_group(jg, carry):
        base = jg * FAR_UNROLL
        ss = logits(base)
        for u in range(FAR_UNROLL):
            nxt = logits(base + u + 1) if u + 1 < FAR_UNROLL else None
            absorb_far(base + u, ss)
            ss = nxt
        return carry

    def far_single(j, carry):
        absorb_far(j, logits(j))
        return carry

    lax.fori_loop(0, n_groups, far_group, 0)
    lax.fori_loop(n_groups * FAR_UNROLL, n_far, far_single, 0)

    @pl.when(i >= 1)
    def _last_two():
        ss_prev = logits(i - 1)
        ss_diag = logits(i)
        absorb_prev(i - 1, ss_prev)
        absorb_diag(i, ss_diag)

    @pl.when(i == 0)
    def _only_diag():
        absorb_diag(i, logits(i))


def _group_mean_matrix(groups, width):
    idx = []
    for gi, size in enumerate(groups):
        idx += [gi] * size
    reps = width // len(idx)
    gid = jnp.asarray([r * len(groups) + g for r in range(reps) for g in idx], jnp.int32)
    sizes = jnp.asarray([float(groups[g]) for _ in range(reps) for g in idx], F32)
    same = gid[:, None] == gid[None, :]
    return jnp.where(same, 1.0 / sizes[None, :], 0.0).astype(BF16)


def _diff_proj_kernel(x_ref, g_ref, wqt_ref, wk_ref, wvt_ref, gq_ref, gk_ref, gmat_ref,
                      qt_ref, k_ref, vt_ref):
    h = _rms_bf16(x_ref[0], g_ref[...])
    yqt = _dot_nt(wqt_ref[...], h)
    for r in range(yqt.shape[0] // DIFF_QK_DIM):
        sl = slice(r * DIFF_QK_DIM, (r + 1) * DIFF_QK_DIM)
        qt_ref[0, 0, sl, :] = _rows_rms(yqt[sl, :], gq_ref[sl, :]).astype(BF16)
    yk = jnp.dot(h, wk_ref[...], preferred_element_type=F32)
    width = gmat_ref.shape[0]
    for c in range(yk.shape[1] // width):
        sl = slice(c * width, (c + 1) * width)
        y = yk[:, sl]
        ms = _split_dot(y * y, gmat_ref[...])
        k_ref[0, :, sl] = (y * lax.rsqrt(ms + RMS_EPS) * gk_ref[:, sl]).astype(BF16)
    vt_ref[0, 0] = _dot_nt(wvt_ref[...], h).astype(BF16)


def _diff_attn_kernel(tab_ref, scal_ref, qt_ref, k_ref, vt_ref, bucket_ref, subg_ref, o_ref,
                      bias_ref, m_ref, l_ref, acc_ref):
    t = qt_ref.shape[3]
    hd = pl.program_id(1)
    i = pl.program_id(2)

    @pl.when(i == 0)
    def _build_bias():
        far = tab_ref[REL_BUCKETS // 2 - 1, hd]
        keys = lax.broadcasted_iota(jnp.int32, (t, t), 0)
        queries = lax.broadcasted_iota(jnp.int32, (t, t), 1)
        visible = (keys // CHUNK) <= (queries // CHUNK)
        for tile in range(2):
            bk = bucket_ref[tile]
            bias = jnp.zeros((t, t), F32)
            for b in range(REL_BUCKETS):
                bias = jnp.where(bk == b, (tab_ref[b, hd] - far) * LOG2E, bias)
            if tile == 1:
                bias = jnp.where(visible, bias, MASK_VALUE)
            bias_ref[tile] = bias

    m_ref[...] = jnp.full(m_ref.shape, -jnp.inf, F32)
    l_ref[...] = jnp.zeros(l_ref.shape, F32)
    acc_ref[...] = jnp.zeros(acc_ref.shape, F32)

    qt = qt_ref[0, 0]
    feat = lax.broadcasted_iota(jnp.int32, qt.shape, 0)
    qmaps = [jnp.where((feat // DIFF_QK_DIM) == mp, qt, jnp.zeros_like(qt)) for mp in range(2)]

    def logits(j):
        k = k_ref[0, pl.ds(pl.multiple_of(j * t, t), t), :]
        return [jnp.dot(k, qmaps[mp], preferred_element_type=F32) for mp in range(2)]

    def absorb(j, ss, bias_tile):
        vt = vt_ref[0, j]
        for mp in range(2):
            s = ss[mp]
            if bias_tile is not None:
                s = s + bias_ref[bias_tile]
            _softmax_step_t(s, vt, m_ref.at[mp], l_ref.at[mp], acc_ref.at[mp])

    _sweep_key_tiles(i, logits,
                     lambda j, ss: absorb(j, ss, None),
                     lambda j, ss: absorb(j, ss, 0),
                     lambda j, ss: absorb(j, ss, 1))

    lam = scal_ref[0]
    attn = acc_ref[0] / l_ref[0] - lam * (acc_ref[1] / l_ref[1])
    o_ref[0] = _rows_rms(attn, subg_ref[...]).T.astype(BF16)


def _t5_bucket(rel):
    nb = REL_BUCKETS // 2
    bucket = jnp.where(rel > 0, nb, 0)
    n = jnp.abs(rel)
    max_exact = nb // 2
    n_f = jnp.maximum(n, 1).astype(jnp.float32)
    large = max_exact + (jnp.log(n_f / max_exact) / math.log(REL_MAX_DIST / max_exact)
                         * (nb - max_exact)).astype(jnp.int32)
    large = jnp.minimum(large, nb - 1)
    return bucket + jnp.where(n < max_exact, n, large)


def _diff_mixer(x3, layer_idx, g, w_in, q_norm_g, k_norm_g, lam, sub_norm_g, w_out, rel_table):
    b, s, d = x3.shape
    t = TOK_TILE
    nb = s // t
    hh, dq, dv = DIFF_HEADS, DIFF_QK_DIM, DIFF_V_DIM
    w3 = w_in.reshape(d, hh, 4 * dq + dv)
    wqt = w3[:, :, :2 * dq].reshape(d, hh * 2 * dq).T.astype(BF16)
    wk = w3[:, :, 2 * dq:4 * dq].reshape(d, hh * 2 * dq).astype(BF16)
    wvt = w3[:, :, 4 * dq:].reshape(d, hh * dv).T.astype(BF16)
    scale = dq ** -0.5
    gq = (jnp.tile(q_norm_g, 2 * hh) * (scale * LOG2E)).reshape(hh * 2 * dq, 1)
    gk = jnp.tile(k_norm_g, 2 * hh).reshape(1, hh * 2 * dq)
    gmat = _group_mean_matrix([dq], 2 * LANES)
    feat_major = lambda w: pl.BlockSpec((1, 1, w, t), lambda bi, i: (bi, i, 0, 0))
    qt, k, vt = pl.pallas_call(
        _diff_proj_kernel,
        grid=(b, nb),
        in_specs=[pl.BlockSpec((1, t, d), lambda bi, i: (bi, i, 0)),
                  _const_spec((1, d)),
                  _const_spec((hh * 2 * dq, d)),
                  _const_spec((d, hh * 2 * dq)),
                  _const_spec((hh * dv, d)),
                  _const_spec((hh * 2 * dq, 1)),
                  _const_spec((1, hh * 2 * dq)),
                  _const_spec((2 * LANES, 2 * LANES))],
        out_specs=[feat_major(hh * 2 * dq),
                   pl.BlockSpec((1, t, hh * 2 * dq), lambda bi, i: (bi, i, 0)),
                   feat_major(hh * dv)],
        out_shape=[jax.ShapeDtypeStruct((b, nb, hh * 2 * dq, t), BF16),
                   jax.ShapeDtypeStruct((b, s, hh * 2 * dq), BF16),
                   jax.ShapeDtypeStruct((b, nb, hh * dv, t), BF16)],
        compiler_params=_params(("parallel", "parallel")),
        name="diff_proj",
    )(x3, g.reshape(1, d), wqt, wk, wvt, gq, gk, gmat)

    lam_init = LAMBDA_INIT_BASE - LAMBDA_INIT_SCALE * math.exp(-LAMBDA_INIT_DECAY * layer_idx)
    lam_full = (jnp.exp(jnp.sum(lam[0] * lam[1]).astype(F32))
                - jnp.exp(jnp.sum(lam[2] * lam[3]).astype(F32)) + lam_init)
    scal = jnp.reshape(lam_full, (1,)).astype(F32)
    r = jnp.arange(t)
    rel_diag = r[:, None] - r[None, :]
    buckets = jnp.stack([_t5_bucket(rel_diag - t), _t5_bucket(rel_diag)]).astype(jnp.int32)
    subg = (sub_norm_g * (1.0 - lam_init)).reshape(dv, 1)
    smem = pl.BlockSpec(memory_space=pltpu.SMEM)
    o = pl.pallas_call(
        _diff_attn_kernel,
        grid=(b, hh, nb),
        in_specs=[smem, smem,
                  pl.BlockSpec((1, 1, 2 * dq, t), lambda bi, h, i: (bi, i, h, 0)),
                  pl.BlockSpec((1, s, 2 * dq), lambda bi, h, i: (bi, 0, h)),
                  pl.BlockSpec((1, nb, dv, t), lambda bi, h, i: (bi, 0, h, 0)),
                  _const_spec((2, t, t)),
                  _const_spec((dv, 1))],
        out_specs=pl.BlockSpec((1, t, dv), lambda bi, h, i: (bi, i, h)),
        out_shape=jax.ShapeDtypeStruct((b, s, hh * dv), BF16),
        scratch_shapes=[pltpu.VMEM((2, t, t), F32),
                        pltpu.VMEM((2, 1, t), F32),
                        pltpu.VMEM((2, 1, t), F32),
                        pltpu.VMEM((2, dv, t), F32)],
        compiler_params=_params(("parallel", "parallel", "arbitrary")),
        name="diff_attn",
    )(rel_table.astype(F32), scal, qt, k, vt, buckets, subg)
    return _out_proj(x3.reshape(b * s, d), o.reshape(b * s, hh * dv), w_out).reshape(b, s, d)


def _sb_proj_kernel(x_ref, g_ref, wqt_ref, wk_ref, wvt_ref, qt_ref, k_ref, vt_ref):
    h = _rms_bf16(x_ref[0], g_ref[...])
    qt_ref[0, 0] = (_dot_nt(wqt_ref[...], h) * (SB_HEAD_DIM ** -0.5)).astype(BF16)
    k_ref[0] = jnp.dot(h, wk_ref[...], preferred_element_type=F32).astype(BF16)
    yvt = _dot_nt(wvt_ref[...], h).astype(BF16)
    sk = vt_ref.shape[3]
    for c in range(vt_ref.shape[1]):
        vt_ref[0, c] = yvt[:, c * sk:(c + 1) * sk]


def _sb_attn_kernel(qt_ref, k_ref, vt_ref, tri_ref, o_ref, run_ref, acc_ref):
    t = qt_ref.shape[3]
    kb = SB_KEY_BLOCK
    sk = SB_SWEEP_KEYS
    dh = SB_HEAD_DIM
    i = pl.program_id(2)
    sub = sk // kb
    qt = qt_ref[0, 0]
    feat = lax.broadcasted_iota(jnp.int32, qt.shape, 0)
    qh = [jnp.where((feat // dh) == a, qt, jnp.zeros_like(qt)) for a in range(2)]
    run_ref[...] = jnp.zeros(run_ref.shape, F32)
    acc_ref[...] = jnp.zeros(acc_ref.shape, F32)

    def sweep(j, diagonal, c0):
        qs = slice(c0, t)
        k = k_ref[0, pl.ds(pl.multiple_of(j * sk, sk), sk), :]
        vt = vt_ref[0, j]
        zs = [jnp.dot(k, qh[a][:, qs], preferred_element_type=F32) for a in range(2)]
        if diagonal:
            causal = (lax.broadcasted_iota(jnp.int32, zs[0].shape, 0)
                      < lax.broadcasted_iota(jnp.int32, zs[0].shape, 1))
        for a in range(2):
            z = zs[a]
            soft = jnp.log(1.0 + jnp.exp(-jnp.abs(z)))
            log_beta = jnp.minimum(z, 0.0) - soft
            log_keep = log_beta - z
            if diagonal:
                log_keep = jnp.where(causal, log_keep, 0.0)
            hi = log_keep.astype(BF16)
            lo = (log_keep - hi.astype(F32)).astype(BF16)
            run = run_ref[a, :, qs]
            ws = [None] * sub
            for blk in range(sub - 1, -1, -1):
                rows = slice(blk * kb, (blk + 1) * kb)
                sums = jnp.dot(tri_ref[...], jnp.concatenate([hi[rows], lo[rows]], axis=0),
                               preferred_element_type=F32)
                ws[blk] = jnp.exp(log_beta[rows] + sums[:kb] + run[0:1])
                run = run + sums[kb:]
            w = jnp.concatenate(ws, axis=0)
            if diagonal:
                w = jnp.where(causal, w, 0.0)
            acc_ref[a, :, qs] += jnp.dot(vt[a * dh:(a + 1) * dh, :], w.astype(BF16),
                                         preferred_element_type=F32)
            run_ref[a, :, qs] = run

    per_tile = t // sk
    for b in range(per_tile - 1, -1, -1):
        sweep(i * per_tile + b, True, b * sk)

    def cond(c):
        j, top = c
        return jnp.logical_and(j >= 0, top > F32_EXP_ZERO)

    def body(c):
        j, _ = c
        sweep(j, False, 0)
        return j - 1, jnp.max(run_ref[...])

    lax.while_loop(cond, body, (i * per_tile - 1, jnp.max(run_ref[...])))
    o_ref[0] = jnp.concatenate([acc_ref[0], acc_ref[1]], axis=0).T.astype(BF16)


def _sb_mixer(x3, g, w_in, w_out):
    b, s, d = x3.shape
    t = TOK_TILE
    kb = SB_KEY_BLOCK
    sweep = SB_SWEEP_KEYS
    nb = s // t
    hh, dh = SB_HEADS, SB_HEAD_DIM
    w3 = w_in.reshape(d, hh, 3 * dh)
    wqt = w3[:, :, :dh].reshape(d, hh * dh).T.astype(BF16)
    wk = w3[:, :, dh:2 * dh].reshape(d, hh * dh).astype(BF16)
    wvt = w3[:, :, 2 * dh:].reshape(d, hh * dh).T.astype(BF16)
    qt, k, vt = pl.pallas_call(
        _sb_proj_kernel,
        grid=(b, nb),
        in_specs=[pl.BlockSpec((1, t, d), lambda bi, i: (bi, i, 0)),
                  _const_spec((1, d)),
                  _const_spec((hh * dh, d)),
                  _const_spec((d, hh * dh)),
                  _const_spec((hh * dh, d))],
        out_specs=[pl.BlockSpec((1, 1, hh * dh, t), lambda bi, i: (bi, i, 0, 0)),
                   pl.BlockSpec((1, t, hh * dh), lambda bi, i: (bi, i, 0)),
                   pl.BlockSpec((1, t // sweep, hh * dh, sweep), lambda bi, i: (bi, i, 0, 0))],
        out_shape=[jax.ShapeDtypeStruct((b, nb, hh * dh, t), BF16),
                   jax.ShapeDtypeStruct((b, s, hh * dh), BF16),
                   jax.ShapeDtypeStruct((b, s // sweep, hh * dh, sweep), BF16)],
        compiler_params=_params(("parallel", "parallel")),
        name="sb_proj",
    )(x3, g.reshape(1, d), wqt, wk, wvt)

    rj = jnp.arange(kb + SUBLANES)
    sk = jnp.arange(2 * kb) % kb
    tri = jnp.where((rj[:, None] >= kb) | (sk[None, :] > rj[:, None]), 1.0, 0.0).astype(BF16)
    pair = 2 * dh
    o = pl.pallas_call(
        _sb_attn_kernel,
        grid=(b, hh // 2, nb),
        in_specs=[pl.BlockSpec((1, 1, pair, t), lambda bi, h, i: (bi, i, h, 0)),
                  pl.BlockSpec((1, s, pair), lambda bi, h, i: (bi, 0, h)),
                  pl.BlockSpec((1, s // sweep, pair, sweep), lambda bi, h, i: (bi, 0, h, 0)),
                  _const_spec((kb + SUBLANES, 2 * kb))],
        out_specs=pl.BlockSpec((1, t, pair), lambda bi, h, i: (bi, i, h)),
        out_shape=jax.ShapeDtypeStruct((b, s, hh * dh), BF16),
        scratch_shapes=[pltpu.VMEM((2, SUBLANES, t), F32), pltpu.VMEM((2, dh, t), F32)],
        compiler_params=_params(("parallel", "parallel", "arbitrary")),
        name="sb_attn",
    )(qt, k, vt, tri)
    return _out_proj(x3.reshape(b * s, d), o.reshape(b * s, hh * dh), w_out).reshape(b, s, d)


MLA_QK_PAD = MLA_NOPE + 2 * MLA_ROPE


def _mla_proj_kernel(x_ref, g_ref, waq_ref, wakv_ref, war_ref, gqa_ref, gkva_ref, wuqt_ref,
                     wuk_ref, wuvt_ref, gmat_ref, gq_ref, cst_ref, gkn_ref, gkr_ref, csk_ref,
                     qt_ref, k_ref, vt_ref):
    nope, rope, qp = MLA_NOPE, MLA_ROPE, MLA_QK_PAD
    h = _rms_bf16(x_ref[0], g_ref[...])
    cq = _rms_bf16(jnp.dot(h, waq_ref[...], preferred_element_type=F32), gqa_ref[...])
    ckv = _rms_bf16(jnp.dot(h, wakv_ref[...], preferred_element_type=F32), gkva_ref[...])
    yqt = _dot_nt(wuqt_ref[...], cq)
    cs = cst_ref[...]
    for hd in range(MLA_HEADS):
        base = hd * qp
        y = yqt[base:base + qp, :]
        gcol = gq_ref[base:base + qp, :]
        qt_ref[0, 0, base:base + nope, :] = _rows_rms(y[:nope], gcol[:nope]).astype(BF16)
        msr = jnp.mean(y[nope:nope + rope] * y[nope:nope + rope], axis=0, keepdims=True)
        rr = y[nope:] * lax.rsqrt(msr + RMS_EPS) * gcol[nope:] * cs
        qf = (rr[:rope] + rr[rope:]).astype(BF16)
        qt_ref[0, 0, base + nope:base + nope + rope, :] = qf
        qt_ref[0, 0, base + nope + rope:base + qp, :] = qf
    width = gmat_ref.shape[0]
    gm = gmat_ref[...]
    sh = jnp.dot(h, war_ref[...], preferred_element_type=F32)
    ms = _split_dot(sh * sh, gm[:qp, :qp])
    shn = sh * lax.rsqrt(ms + RMS_EPS) * gkr_ref[...] * csk_ref[...]
    shared = jnp.concatenate([shn] * (width // qp), axis=1)
    ykn = jnp.dot(ckv, wuk_ref[...], preferred_element_type=F32)
    for c in range(ykn.shape[1] // width):
        sl = slice(c * width, (c + 1) * width)
        y = ykn[:, sl]
        ms = _split_dot(y * y, gm)
        k_ref[0, :, sl] = (y * lax.rsqrt(ms + RMS_EPS) * gkn_ref[:, sl] + shared).astype(BF16)
    vt_ref[0, 0] = _dot_nt(wuvt_ref[...], ckv).astype(BF16)


def _mla_attn_kernel(qt_ref, k_ref, vt_ref, o_ref, m_ref, l_ref, acc_ref):
    t = qt_ref.shape[3]
    qp, dv = MLA_QK_PAD, MLA_V
    i = pl.program_id(2)
    m_ref[...] = jnp.full(m_ref.shape, -jnp.inf, F32)
    l_ref[...] = jnp.zeros(l_ref.shape, F32)
    acc_ref[...] = jnp.zeros(acc_ref.shape, F32)

    def logits(j):
        k = k_ref[0, pl.ds(pl.multiple_of(j * t, t), t), :]
        return [jnp.dot(k[:, a * qp:(a + 1) * qp], qt_ref[0, 0, a * qp:(a + 1) * qp, :],
                        preferred_element_type=F32) for a in range(2)]

    def absorb(j, ss, masked):
        vt = vt_ref[0, j]
        for a in range(2):
            s = ss[a]
            if masked:
                keys = lax.broadcasted_iota(jnp.int32, (t, t), 0)
                queries = lax.broadcasted_iota(jnp.int32, (t, t), 1)
                s = jnp.where((keys // CHUNK) <= (queries // CHUNK), s, MASK_VALUE)
            _softmax_step_t(s, vt[a * dv:(a + 1) * dv, :], m_ref.at[a], l_ref.at[a], acc_ref.at[a])

    _sweep_key_tiles(i, logits,
                     lambda j, ss: absorb(j, ss, False),
                     lambda j, ss: absorb(j, ss, False),
                     lambda j, ss: absorb(j, ss, True))
    o = jnp.concatenate([acc_ref[0] / l_ref[0], acc_ref[1] / l_ref[1]], axis=0)
    o_ref[0] = o.T.astype(BF16)


def _rotate_half_cols(w):
    half = MLA_ROPE // 2
    return jnp.concatenate([-w[..., half:], w[..., :half]], axis=-1)


def _mla_mixer(x3, g, w_a, q_a_norm, kv_a_norm, w_uq, w_ukv, q_norm_g, k_norm_g, w_out):
    b, s, d = x3.shape
    t = TOK_TILE
    nb = s // t
    hh = MLA_HEADS
    nope, rope, qp, dv = MLA_NOPE, MLA_ROPE, MLA_QK_PAD, MLA_V
    half = rope // 2
    wa_q = w_a[:, :MLA_Q_LORA].astype(BF16)
    wa_kv = w_a[:, MLA_Q_LORA:MLA_Q_LORA + MLA_KV_LORA].astype(BF16)
    wa_r = w_a[:, MLA_Q_LORA + MLA_KV_LORA:]
    war = jnp.concatenate([jnp.zeros((d, nope), F32), wa_r, _rotate_half_cols(wa_r)], axis=1).astype(BF16)
    wq3 = w_uq.reshape(MLA_Q_LORA, hh, nope + rope)
    wuqt = jnp.concatenate([wq3, _rotate_half_cols(wq3[:, :, nope:])], axis=2
                           ).reshape(MLA_Q_LORA, hh * qp).T.astype(BF16)
    wkv3 = w_ukv.reshape(MLA_KV_LORA, hh, nope + dv)
    wuk = jnp.concatenate([wkv3[:, :, :nope], jnp.zeros((MLA_KV_LORA, hh, 2 * rope), F32)], axis=2
                          ).reshape(MLA_KV_LORA, hh * qp).astype(BF16)
    wuvt = wkv3[:, :, nope:].reshape(MLA_KV_LORA, hh * dv).T.astype(BF16)
    swap = lambda v: jnp.concatenate([v[half:], v[:half]])
    scale = (nope + rope) ** -0.5
    gq_head = jnp.concatenate([q_norm_g, swap(q_norm_g[nope:])]) * (scale * LOG2E)
    gq = jnp.tile(gq_head, hh).reshape(hh * qp, 1)
    gkn = jnp.tile(jnp.concatenate([k_norm_g[:nope], jnp.zeros((2 * rope,), F32)]), hh).reshape(1, hh * qp)
    gkr = jnp.concatenate([jnp.zeros((nope,), F32), k_norm_g[nope:], swap(k_norm_g[nope:])]).reshape(1, qp)
    inv = ROPE_BASE ** (-jnp.arange(0, rope, 2, dtype=F32) / rope)
    ang = jnp.arange(s, dtype=F32)[:, None] * inv[None, :]
    cos, sin = jnp.cos(ang), jnp.sin(ang)
    cs = jnp.concatenate([cos, cos, sin, sin], axis=1)
    csk = jnp.concatenate([jnp.zeros((s, nope), F32), cs], axis=1)
    gmat = _group_mean_matrix([nope, rope, rope], 2 * qp)
    feat_major = lambda w: pl.BlockSpec((1, 1, w, t), lambda bi, i: (bi, i, 0, 0))
    qt, k, vt = pl.pallas_call(
        _mla_proj_kernel,
        grid=(b, nb),
        in_specs=[pl.BlockSpec((1, t, d), lambda bi, i: (bi, i, 0)),
                  _const_spec((1, d)),
                  _const_spec((d, MLA_Q_LORA)),
                  _const_spec((d, MLA_KV_LORA)),
                  _const_spec((d, qp)),
                  _const_spec((1, MLA_Q_LORA)),
                  _const_spec((1, MLA_KV_LORA)),
                  _const_spec((hh * qp, MLA_Q_LORA)),
                  _const_spec((MLA_KV_LORA, hh * qp)),
                  _const_spec((hh * dv, MLA_KV_LORA)),
                  _const_spec((2 * qp, 2 * qp)),
                  _const_spec((hh * qp, 1)),
                  pl.BlockSpec((2 * rope, t), lambda bi, i: (0, i)),
                  _const_spec((1, hh * qp)),
                  _const_spec((1, qp)),
                  pl.BlockSpec((t, qp), lambda bi, i: (i, 0))],
        out_specs=[feat_major(hh * qp),
                   pl.BlockSpec((1, t, hh * qp), lambda bi, i: (bi, i, 0)),
                   feat_major(hh * dv)],
        out_shape=[jax.ShapeDtypeStruct((b, nb, hh * qp, t), BF16),
                   jax.ShapeDtypeStruct((b, s, hh * qp), BF16),
                   jax.ShapeDtypeStruct((b, nb, hh * dv, t), BF16)],
        compiler_params=_params(("parallel", "parallel")),
        name="mla_proj",
    )(x3, g.reshape(1, d), wa_q, wa_kv, war, q_a_norm.reshape(1, -1), kv_a_norm.reshape(1, -1),
      wuqt, wuk, wuvt, gmat, gq, cs.T, gkn, gkr, csk)

    o = pl.pallas_call(
        _mla_attn_kernel,
        grid=(b, hh // 2, nb),
        in_specs=[pl.BlockSpec((1, 1, 2 * qp, t), lambda bi, h, i: (bi, i, h, 0)),
                  pl.BlockSpec((1, s, 2 * qp), lambda bi, h, i: (bi, 0, h)),
                  pl.BlockSpec((1, nb, 2 * dv, t), lambda bi, h, i: (bi, 0, h, 0))],
        out_specs=pl.BlockSpec((1, t, 2 * dv), lambda bi, h, i: (bi, i, h)),
        out_shape=jax.ShapeDtypeStruct((b, s, hh * dv), BF16),
        scratch_shapes=[pltpu.VMEM((2, 1, t), F32), pltpu.VMEM((2, 1, t), F32),
                        pltpu.VMEM((2, dv, t), F32)],
        compiler_params=_params(("parallel", "parallel", "arbitrary")),
        name="mla_attn",
    )(qt, k, vt)
    return _out_proj(x3.reshape(b * s, d), o.reshape(b * s, hh * dv), w_out).reshape(b, s, d)


def kernel(x, rel_bias_table, ffn_norm, ffn_w_in, ffn_w_out, mixer_norm, conv_w_in, conv_b_in, conv_w_dw, conv_b_dw, conv_ln_g, conv_ln_b, conv_w_out, conv_b_out, diff_w_in, diff_q_norm, diff_k_norm, diff_lambda, diff_sub_norm, diff_w_out, sb_w_in, sb_w_out, mla_w_a, mla_q_a_norm, mla_kv_a_norm, mla_w_uq, mla_w_ukv, mla_q_norm, mla_k_norm, mla_w_out):
    b, s, d = x.shape
    assert d == D_MODEL and s % TOK_TILE == 0
    depth = ffn_norm.shape[0]
    ffn = lambda xx, i, k: _ffn(xx.reshape(b * s, d), ffn_norm[i, k], ffn_w_in[i, k],
                                ffn_w_out[i, k]).reshape(b, s, d)
    for i in range(depth):
        mixer, j = i % N_MIXERS, i // N_MIXERS
        x = ffn(x, i, 0)
        g = mixer_norm[i]
        if mixer == 0:
            x = _conv_mixer(x, g, conv_w_in[j], conv_b_in[j], conv_w_dw[j], conv_b_dw[j],
                            conv_ln_g[j], conv_ln_b[j], conv_w_out[j], conv_b_out[j])
        elif mixer == 1:
            x = _diff_mixer(x, i, g, diff_w_in[j], diff_q_norm[j], diff_k_norm[j], diff_lambda[j],
                            diff_sub_norm[j], diff_w_out[j], rel_bias_table)
        elif mixer == 2:
            x = _sb_mixer(x, g, sb_w_in[j], sb_w_out[j])
        else:
            x = _mla_mixer(x, g, mla_w_a[j], mla_q_a_norm[j], mla_kv_a_norm[j], mla_w_uq[j],
                           mla_w_ukv[j], mla_q_norm[j], mla_k_norm[j], mla_w_out[j])
        x = ffn(x, i, 1)
    return x
```

```python
import math

import jax
import jax.numpy as jnp
from jax import lax
from jax.experimental import pallas as pl
from jax.experimental.pallas import tpu as pltpu

F32 = jnp.float32
BF16 = jnp.bfloat16

D_MODEL = 1024
DEPTH = 4
CHUNK = 64
N_MIXERS = 4
D_FF = 2816
RMS_EPS = 1e-6
LN_EPS = 1e-5
MASK_VALUE = -1e30
CONV_WIDTH = 31
DIFF_HEADS = 8
DIFF_QK_DIM = 64
DIFF_V_DIM = 128
LAMBDA_INIT_BASE = 0.8
LAMBDA_INIT_SCALE = 0.6
LAMBDA_INIT_DECAY = 0.3
REL_BUCKETS = 32
REL_MAX_DIST = 128
SB_HEADS = 16
SB_HEAD_DIM = 64
MLA_HEADS = 16
MLA_Q_LORA = 384
MLA_KV_LORA = 256
MLA_NOPE = 64
MLA_ROPE = 32
MLA_V = 64
ROPE_BASE = 10000.0
LOG2E = math.log2(math.e)

LANES = 128
SUBLANES = 8
TOK_TILE = 512
FF_CHUNK = 256
FAR_UNROLL = 4
SB_KEY_BLOCK = 128
SB_SWEEP_KEYS = 256
CONV_HALO = 32
CONV_ROWS = 32
VMEM_LIMIT = 56 * 1024 * 1024
F32_EXP2_ZERO = -151.0


def _params(sem):
    return pltpu.CompilerParams(dimension_semantics=sem, vmem_limit_bytes=VMEM_LIMIT)


def _const_spec(shape):
    nd = len(shape)
    return pl.BlockSpec(shape, lambda *_: (0,) * nd, pipeline_mode=pl.Buffered(1))


def _rms_bf16(x, g):
    ms = jnp.mean(x * x, axis=-1, keepdims=True)
    return (x * lax.rsqrt(ms + RMS_EPS) * g).astype(BF16)


def _split_dot(x, w):
    hi = x.astype(BF16)
    lo = (x - hi.astype(F32)).astype(BF16)
    return (jnp.dot(hi, w, preferred_element_type=F32)
            + jnp.dot(lo, w, preferred_element_type=F32))


def _dot_nt(a, b):
    return lax.dot_general(a, b, (((1,), (1,)), ((), ())), preferred_element_type=F32)


def _rows_rms(y, g_col):
    ms = jnp.mean(y * y, axis=0, keepdims=True)
    return y * lax.rsqrt(ms + RMS_EPS) * g_col


def _ffn_kernel(x_ref, g_ref, win_ref, wout_ref, o_ref, gate_ref):
    x = x_ref[...]
    h = _rms_bf16(x, g_ref[...])
    for c in range(D_FF // FF_CHUNK):
        lo, hi = c * FF_CHUNK, (c + 1) * FF_CHUNK
        a = jnp.dot(h, win_ref[:, lo:hi], preferred_element_type=F32)
        u = jnp.dot(h, win_ref[:, D_FF + lo:D_FF + hi], preferred_element_type=F32)
        gate_ref[:, lo:hi] = (a * jax.nn.sigmoid(a) * u).astype(BF16)
    y = jnp.dot(gate_ref[...], wout_ref[...], preferred_element_type=F32)
    o_ref[...] = x + 0.5 * y


def _ffn(x2, g, w_in, w_out):
    n, d = x2.shape
    tm = TOK_TILE
    return pl.pallas_call(
        _ffn_kernel,
        grid=(n // tm,),
        in_specs=[pl.BlockSpec((tm, d), lambda i: (i, 0)),
                  _const_spec((1, d)),
                  _const_spec((d, 2 * D_FF)),
                  _const_spec((D_FF, d))],
        out_specs=pl.BlockSpec((tm, d), lambda i: (i, 0)),
        out_shape=jax.ShapeDtypeStruct((n, d), F32),
        scratch_shapes=[pltpu.VMEM((tm, D_FF), BF16)],
        compiler_params=_params(("parallel",)),
        name="ffn",
    )(x2, g.reshape(1, d), w_in.astype(BF16), w_out.astype(BF16))


def _out_proj_kernel(x_ref, o_ref, w_ref, y_ref):
    y_ref[...] = x_ref[...] + jnp.dot(o_ref[...], w_ref[...], preferred_element_type=F32)


def _out_proj(x2, o2, w):
    n, d = x2.shape
    kd = o2.shape[1]
    tm = TOK_TILE
    return pl.pallas_call(
        _out_proj_kernel,
        grid=(n // tm,),
        in_specs=[pl.BlockSpec((tm, d), lambda i: (i, 0)),
                  pl.BlockSpec((tm, kd), lambda i: (i, 0)),
                  _const_spec((kd, d))],
        out_specs=pl.BlockSpec((tm, d), lambda i: (i, 0)),
        out_shape=jax.ShapeDtypeStruct((n, d), F32),
        compiler_params=_params(("parallel",)),
        name="out_proj",
    )(x2, o2, w.astype(BF16))


def _conv_in_kernel(x_ref, g_ref, w_ref, b_ref, u_ref):
    d = D_MODEL
    h = _rms_bf16(x_ref[...], g_ref[...])
    y = jnp.dot(h, w_ref[...], preferred_element_type=F32) + b_ref[...]
    u_ref[...] = y[:, :d] * jax.nn.sigmoid(y[:, d:])


def _conv_out_kernel(x_ref, ucur_ref, uprev_ref, wdw_ref, bdw_ref, lng_ref, lnb_ref,
                     wout_ref, bout_ref, y_ref, ext_ref, conv_ref):
    ts = ucur_ref.shape[1]
    i = pl.program_id(1)
    ext_ref[0, 0:CONV_HALO, :] = jnp.where(i > 0, uprev_ref[0], 0.0)
    ext_ref[0, CONV_HALO:, :] = ucur_ref[0]
    n_shift = ts + CONV_HALO - SUBLANES
    for p in range(1, SUBLANES):
        ext_ref[p, 0:n_shift, :] = ext_ref[0, p:p + n_shift, :]
    off = CONV_HALO - (CONV_WIDTH - 1)
    for r in range(ts // CONV_ROWS):
        r0 = r * CONV_ROWS
        acc = None
        for k in range(CONV_WIDTH):
            p = (off + k) % SUBLANES
            a = r0 + off + k - p
            tap = ext_ref[p, a:a + CONV_ROWS, :] * wdw_ref[k:k + 1, :]
            acc = tap if acc is None else acc + tap
        conv_ref[r0:r0 + CONV_ROWS, :] = acc + bdw_ref[...]
    c = conv_ref[...]
    mu = jnp.mean(c, axis=-1, keepdims=True)
    cc = c - mu
    var = jnp.mean(cc * cc, axis=-1, keepdims=True)
    ln = cc * lax.rsqrt(var + LN_EPS) * lng_ref[...] + lnb_ref[...]
    act = (ln * jax.nn.sigmoid(ln)).astype(BF16)
    y = jnp.dot(act, wout_ref[...], preferred_element_type=F32) + bout_ref[...]
    y_ref[0] = x_ref[0] + y


def _conv_mixer(x3, g, w_in, b_in, w_dw, b_dw, ln_g, ln_b, w_out, b_out):
    b, s, d = x3.shape
    n = b * s
    tm = TOK_TILE
    u = pl.pallas_call(
        _conv_in_kernel,
        grid=(n // tm,),
        in_specs=[pl.BlockSpec((tm, d), lambda i: (i, 0)),
                  _const_spec((1, d)),
                  _const_spec((d, 2 * d)),
                  _const_spec((1, 2 * d))],
        out_specs=pl.BlockSpec((tm, d), lambda i: (i, 0)),
        out_shape=jax.ShapeDtypeStruct((n, d), F32),
        compiler_params=_params(("parallel",)),
        name="conv_in",
    )(x3.reshape(n, d), g.reshape(1, d), w_in.astype(BF16), b_in.reshape(1, 2 * d))
    u3 = u.reshape(b, s, d)
    halo_per_tile = tm // CONV_HALO
    row = lambda v: v.reshape(1, d)
    return pl.pallas_call(
        _conv_out_kernel,
        grid=(b, s // tm),
        in_specs=[pl.BlockSpec((1, tm, d), lambda bi, i: (bi, i, 0)),
                  pl.BlockSpec((1, tm, d), lambda bi, i: (bi, i, 0)),
                  pl.BlockSpec((1, CONV_HALO, d),
                               lambda bi, i: (bi, jnp.maximum(i * halo_per_tile - 1, 0), 0)),
                  _const_spec((CONV_WIDTH, d)),
                  _const_spec((1, d)), _const_spec((1, d)), _const_spec((1, d)),
                  _const_spec((d, d)),
                  _const_spec((1, d))],
        out_specs=pl.BlockSpec((1, tm, d), lambda bi, i: (bi, i, 0)),
        out_shape=jax.ShapeDtypeStruct((b, s, d), F32),
        scratch_shapes=[pltpu.VMEM((SUBLANES, tm + CONV_HALO, d), F32), pltpu.VMEM((tm, d), F32)],
        compiler_params=_params(("parallel", "parallel")),
        name="conv_out",
    )(x3, u3, u3, w_dw, row(b_dw), row(ln_g), row(ln_b), w_out.astype(BF16), row(b_out))


def _stage_logits(s, s_ref, mb_ref):
    s_ref[...] = s
    mb_ref[...] = jnp.max(s, axis=0, keepdims=True)


def _softmax_step_t(s_ref, mb_ref, vt, m_ref, l_ref, acc_ref):
    m_prev = m_ref[...]
    m_new = jnp.maximum(m_prev, mb_ref[...])
    alpha = jnp.exp2(m_prev - m_new)
    p = jnp.exp2(s_ref[...] - m_new)
    l_ref[...] = alpha * l_ref[...] + jnp.sum(p, axis=0, keepdims=True)
    acc_ref[...] = alpha * acc_ref[...] + jnp.dot(vt, p.astype(BF16), preferred_element_type=F32)
    m_ref[...] = m_new


FAR, PREV, DIAG = "far", "prev", "diag"


def _sweep_key_tiles(i, logits, absorb):
    n_far = jnp.maximum(i - 1, 0)
    n_groups = n_far // FAR_UNROLL

    def far_group(jg, carry):
        base = jg * FAR_UNROLL
        logits(base, FAR, 0)
        for u in range(FAR_UNROLL):
            if u + 1 < FAR_UNROLL:
                logits(base + u + 1, FAR, (u + 1) % 2)
            absorb(base + u, u % 2)
        return carry

    def far_single(j, carry):
        logits(j, FAR, 0)
        absorb(j, 0)
        return carry

    lax.fori_loop(0, n_groups, far_group, 0)
    lax.fori_loop(n_groups * FAR_UNROLL, n_far, far_single, 0)

    @pl.when(i >= 1)
    def _last_two():
        logits(i - 1, PREV, 0)
        logits(i, DIAG, 1)
        absorb(i - 1, 0)
        absorb(i, 1)

    @pl.when(i == 0)
    def _only_diag():
        logits(i, DIAG, 0)
        absorb(i, 0)


def _group_mean_matrix(groups, width):
    idx = []
    for gi, size in enumerate(groups):
        idx += [gi] * size
    reps = width // len(idx)
    gid = jnp.asarray([r * len(groups) + g for r in range(reps) for g in idx], jnp.int32)
    sizes = jnp.asarray([float(groups[g]) for _ in range(reps) for g in idx], F32)
    same = gid[:, None] == gid[None, :]
    return jnp.where(same, 1.0 / sizes[None, :], 0.0).astype(BF16)


def _diff_proj_kernel(x_ref, g_ref, wqt_ref, wk_ref, wvt_ref, gq_ref, gk_ref, gmat_ref,
                      qt_ref, k_ref, vt_ref):
    h = _rms_bf16(x_ref[0], g_ref[...])
    yqt = _dot_nt(wqt_ref[...], h)
    for r in range(yqt.shape[0] // DIFF_QK_DIM):
        sl = slice(r * DIFF_QK_DIM, (r + 1) * DIFF_QK_DIM)
        qt_ref[0, 0, sl, :] = _rows_rms(yqt[sl, :], gq_ref[sl, :]).astype(BF16)
    yk = jnp.dot(h, wk_ref[...], preferred_element_type=F32)
    width = gmat_ref.shape[0]
    for c in range(yk.shape[1] // width):
        sl = slice(c * width, (c + 1) * width)
        y = yk[:, sl]
        ms = _split_dot(y * y, gmat_ref[...])
        k_ref[0, :, sl] = (y * lax.rsqrt(ms + RMS_EPS) * gk_ref[:, sl]).astype(BF16)
    vt_ref[0, 0] = _dot_nt(wvt_ref[...], h).astype(BF16)


def _diff_attn_kernel(tab_ref, scal_ref, qt_ref, k_ref, vt_ref, bucket_ref, subg_ref, o_ref,
                      bias_ref, m_ref, l_ref, acc_ref, s_ref, mb_ref):
    t = qt_ref.shape[3]
    hd = pl.program_id(0)
    i = pl.program_id(2)

    @pl.when(jnp.logical_and(pl.program_id(1) == 0, i == 0))
    def _build_bias():
        far = tab_ref[REL_BUCKETS // 2 - 1, hd]
        keys = lax.broadcasted_iota(jnp.int32, (t, t), 0)
        queries = lax.broadcasted_iota(jnp.int32, (t, t), 1)
        visible = (keys // CHUNK) <= (queries // CHUNK)
        for tile in range(2):
            bk = bucket_ref[tile]
            bias = jnp.zeros((t, t), F32)
            for b in range(REL_BUCKETS):
                bias = jnp.where(bk == b, (tab_ref[b, hd] - far) * LOG2E, bias)
            if tile == 1:
                bias = jnp.where(visible, bias, MASK_VALUE)
            bias_ref[tile] = bias

    m_ref[...] = jnp.full(m_ref.shape, -jnp.inf, F32)
    l_ref[...] = jnp.zeros(l_ref.shape, F32)
    acc_ref[...] = jnp.zeros(acc_ref.shape, F32)

    qt = qt_ref[0, 0]
    feat = lax.broadcasted_iota(jnp.int32, qt.shape, 0)
    qmaps = [jnp.where((feat // DIFF_QK_DIM) == mp, qt, jnp.zeros_like(qt)) for mp in range(2)]

    bias_tile = {FAR: None, PREV: 0, DIAG: 1}

    def logits(j, kind, slot):
        k = k_ref[0, pl.ds(pl.multiple_of(j * t, t), t), :]
        ss = [jnp.dot(k, qmaps[mp], preferred_element_type=F32) for mp in range(2)]
        for mp in range(2):
            s = ss[mp] if kind == FAR else ss[mp] + bias_ref[bias_tile[kind]]
            _stage_logits(s, s_ref.at[slot, mp], mb_ref.at[slot, mp])

    def absorb(j, slot):
        vt = vt_ref[0, j]
        for mp in range(2):
            _softmax_step_t(s_ref.at[slot, mp], mb_ref.at[slot, mp], vt,
                            m_ref.at[mp], l_ref.at[mp], acc_ref.at[mp])

    _sweep_key_tiles(i, logits, absorb)

    lam = scal_ref[0]
    attn = acc_ref[0] / l_ref[0] - lam * (acc_ref[1] / l_ref[1])
    o_ref[0] = _rows_rms(attn, subg_ref[...]).T.astype(BF16)


def _t5_bucket(rel):
    nb = REL_BUCKETS // 2
    bucket = jnp.where(rel > 0, nb, 0)
    n = jnp.abs(rel)
    max_exact = nb // 2
    n_f = jnp.maximum(n, 1).astype(jnp.float32)
    large = max_exact + (jnp.log(n_f / max_exact) / math.log(REL_MAX_DIST / max_exact)
                         * (nb - max_exact)).astype(jnp.int32)
    large = jnp.minimum(large, nb - 1)
    return bucket + jnp.where(n < max_exact, n, large)


def _diff_mixer(x3, layer_idx, g, w_in, q_norm_g, k_norm_g, lam, sub_norm_g, w_out, rel_table):
    b, s, d = x3.shape
    t = TOK_TILE
    nb = s // t
    hh, dq, dv = DIFF_HEADS, DIFF_QK_DIM, DIFF_V_DIM
    w3 = w_in.reshape(d, hh, 4 * dq + dv)
    wqt = w3[:, :, :2 * dq].reshape(d, hh * 2 * dq).T.astype(BF16)
    wk = w3[:, :, 2 * dq:4 * dq].reshape(d, hh * 2 * dq).astype(BF16)
    wvt = w3[:, :, 4 * dq:].reshape(d, hh * dv).T.astype(BF16)
    scale = dq ** -0.5
    gq = (jnp.tile(q_norm_g, 2 * hh) * (scale * LOG2E)).reshape(hh * 2 * dq, 1)
    gk = jnp.tile(k_norm_g, 2 * hh).reshape(1, hh * 2 * dq)
    gmat = _group_mean_matrix([dq], 2 * LANES)
    feat_major = lambda w: pl.BlockSpec((1, 1, w, t), lambda bi, i: (bi, i, 0, 0))
    qt, k, vt = pl.pallas_call(
        _diff_proj_kernel,
        grid=(b, nb),
        in_specs=[pl.BlockSpec((1, t, d), lambda bi, i: (bi, i, 0)),
                  _const_spec((1, d)),
                  _const_spec((hh * 2 * dq, d)),
                  _const_spec((d, hh * 2 * dq)),
                  _const_spec((hh * dv, d)),
                  _const_spec((hh * 2 * dq, 1)),
                  _const_spec((1, hh * 2 * dq)),
                  _const_spec((2 * LANES, 2 * LANES))],
        out_specs=[feat_major(hh * 2 * dq),
                   pl.BlockSpec((1, t, hh * 2 * dq), lambda bi, i: (bi, i, 0)),
                   feat_major(hh * dv)],
        out_shape=[jax.ShapeDtypeStruct((b, nb, hh * 2 * dq, t), BF16),
                   jax.ShapeDtypeStruct((b, s, hh * 2 * dq), BF16),
                   jax.ShapeDtypeStruct((b, nb, hh * dv, t), BF16)],
        compiler_params=_params(("parallel", "parallel")),
        name="diff_proj",
    )(x3, g.reshape(1, d), wqt, wk, wvt, gq, gk, gmat)

    lam_init = LAMBDA_INIT_BASE - LAMBDA_INIT_SCALE * math.exp(-LAMBDA_INIT_DECAY * layer_idx)
    lam_full = (jnp.exp(jnp.sum(lam[0] * lam[1]).astype(F32))
                - jnp.exp(jnp.sum(lam[2] * lam[3]).astype(F32)) + lam_init)
    scal = jnp.reshape(lam_full, (1,)).astype(F32)
    r = jnp.arange(t)
    rel_diag = r[:, None] - r[None, :]
    buckets = jnp.stack([_t5_bucket(rel_diag - t), _t5_bucket(rel_diag)]).astype(jnp.int32)
    subg = (sub_norm_g * (1.0 - lam_init)).reshape(dv, 1)
    smem = pl.BlockSpec(memory_space=pltpu.SMEM)
    o = pl.pallas_call(
        _diff_attn_kernel,
        grid=(hh, b, nb),
        in_specs=[smem, smem,
                  pl.BlockSpec((1, 1, 2 * dq, t), lambda h, bi, i: (bi, i, h, 0)),
                  pl.BlockSpec((1, s, 2 * dq), lambda h, bi, i: (bi, 0, h)),
                  pl.BlockSpec((1, nb, dv, t), lambda h, bi, i: (bi, 0, h, 0)),
                  _const_spec((2, t, t)),
                  _const_spec((dv, 1))],
        out_specs=pl.BlockSpec((1, t, dv), lambda h, bi, i: (bi, i, h)),
        out_shape=jax.ShapeDtypeStruct((b, s, hh * dv), BF16),
        scratch_shapes=[pltpu.VMEM((2, t, t), F32),
                        pltpu.VMEM((2, 1, t), F32),
                        pltpu.VMEM((2, 1, t), F32),
                        pltpu.VMEM((2, dv, t), F32),
                        pltpu.VMEM((2, 2, t, t), F32),
                        pltpu.VMEM((2, 2, 1, t), F32)],
        compiler_params=_params(("parallel", "arbitrary", "arbitrary")),
        name="diff_attn",
    )(rel_table.astype(F32), scal, qt, k, vt, buckets, subg)
    return _out_proj(x3.reshape(b * s, d), o.reshape(b * s, hh * dv), w_out).reshape(b, s, d)


def _sb_proj_kernel(x_ref, g_ref, wqt_ref, wk_ref, wvt_ref, qt_ref, k_ref, vt_ref):
    h = _rms_bf16(x_ref[0], g_ref[...])
    qt_ref[0, 0] = (_dot_nt(wqt_ref[...], h) * (SB_HEAD_DIM ** -0.5 * LOG2E)).astype(BF16)
    k_ref[0] = jnp.dot(h, wk_ref[...], preferred_element_type=F32).astype(BF16)
    yvt = _dot_nt(wvt_ref[...], h).astype(BF16)
    sk = vt_ref.shape[3]
    for c in range(vt_ref.shape[1]):
        vt_ref[0, c] = yvt[:, c * sk:(c + 1) * sk]


def _sb_attn_kernel(qt_ref, k_ref, vt_ref, tri_ref, o_ref, run_ref, acc_ref):
    t = qt_ref.shape[3]
    kb = SB_KEY_BLOCK
    sk = SB_SWEEP_KEYS
    dh = SB_HEAD_DIM
    i = pl.program_id(2)
    sub = sk // kb
    qt = qt_ref[0, 0]
    feat = lax.broadcasted_iota(jnp.int32, qt.shape, 0)
    qh = [jnp.where((feat // dh) == a, qt, jnp.zeros_like(qt)) for a in range(2)]
    run_ref[...] = jnp.zeros(run_ref.shape, F32)
    acc_ref[...] = jnp.zeros(acc_ref.shape, F32)

    def logits(j, c0):
        k = k_ref[0, pl.ds(pl.multiple_of(j * sk, sk), sk), :]
        return [jnp.dot(k, qh[a][:, c0:], preferred_element_type=F32) for a in range(2)]

    def fold(j, zs, diagonal, c0):
        qs = slice(c0, t)
        vt = vt_ref[0, j]
        if diagonal:
            causal = (lax.broadcasted_iota(jnp.int32, zs[0].shape, 0)
                      < lax.broadcasted_iota(jnp.int32, zs[0].shape, 1))
        for a in range(2):
            z = zs[a]
            soft = jnp.log2(1.0 + jnp.exp2(-jnp.abs(z)))
            log_beta = jnp.minimum(z, 0.0) - soft
            log_keep = log_beta - z
            if diagonal:
                log_keep = jnp.where(causal, log_keep, 0.0)
            hi = log_keep.astype(BF16)
            lo = (log_keep - hi.astype(F32)).astype(BF16)
            run = run_ref[a, :, qs]
            ws = [None] * sub
            for blk in range(sub - 1, -1, -1):
                rows = slice(blk * kb, (blk + 1) * kb)
                sums = jnp.dot(tri_ref[...], jnp.concatenate([hi[rows], lo[rows]], axis=0),
                               preferred_element_type=F32)
                ws[blk] = jnp.exp2(log_beta[rows] + sums[:kb] + run[0:1])
                run = run + sums[kb:]
            w = jnp.concatenate(ws, axis=0)
            if diagonal:
                w = jnp.where(causal, w, 0.0)
            acc_ref[a, :, qs] += jnp.dot(vt[a * dh:(a + 1) * dh, :], w.astype(BF16),
                                         preferred_element_type=F32)
            run_ref[a, :, qs] = run

    per_tile = t // sk
    own = [(i * per_tile + b, True, b * sk) for b in range(per_tile - 1, -1, -1)]

    def run_blocks(blocks):
        zs = logits(blocks[0][0], blocks[0][2])
        for n, (j, diagonal, c0) in enumerate(blocks):
            nxt = logits(blocks[n + 1][0], blocks[n + 1][2]) if n + 1 < len(blocks) else None
            fold(j, zs, diagonal, c0)
            zs = nxt

    @pl.when(i >= 1)
    def _with_previous():
        run_blocks(own + [(i * per_tile - 1, False, 0)])

    @pl.when(i == 0)
    def _first_tile():
        run_blocks(own)

    def cond(c):
        j, top = c
        return jnp.logical_and(j >= 0, top > F32_EXP2_ZERO)

    def body(c):
        j, _ = c
        fold(j, logits(j, 0), False, 0)
        return j - 1, jnp.max(run_ref[...])

    lax.while_loop(cond, body, (i * per_tile - 2, jnp.max(run_ref[...])))
    o_ref[0] = jnp.concatenate([acc_ref[0], acc_ref[1]], axis=0).T.astype(BF16)


def _sb_mixer(x3, g, w_in, w_out):
    b, s, d = x3.shape
    t = TOK_TILE
    kb = SB_KEY_BLOCK
    sweep = SB_SWEEP_KEYS
    nb = s // t
    hh, dh = SB_HEADS, SB_HEAD_DIM
    w3 = w_in.reshape(d, hh, 3 * dh)
    wqt = w3[:, :, :dh].reshape(d, hh * dh).T.astype(BF16)
    wk = w3[:, :, dh:2 * dh].reshape(d, hh * dh).astype(BF16)
    wvt = w3[:, :, 2 * dh:].reshape(d, hh * dh).T.astype(BF16)
    qt, k, vt = pl.pallas_call(
        _sb_proj_kernel,
        grid=(b, nb),
        in_specs=[pl.BlockSpec((1, t, d), lambda bi, i: (bi, i, 0)),
                  _const_spec((1, d)),
                  _const_spec((hh * dh, d)),
                  _const_spec((d, hh * dh)),
                  _const_spec((hh * dh, d))],
        out_specs=[pl.BlockSpec((1, 1, hh * dh, t), lambda bi, i: (bi, i, 0, 0)),
                   pl.BlockSpec((1, t, hh * dh), lambda bi, i: (bi, i, 0)),
                   pl.BlockSpec((1, t // sweep, hh * dh, sweep), lambda bi, i: (bi, i, 0, 0))],
        out_shape=[jax.ShapeDtypeStruct((b, nb, hh * dh, t), BF16),
                   jax.ShapeDtypeStruct((b, s, hh * dh), BF16),
                   jax.ShapeDtypeStruct((b, s // sweep, hh * dh, sweep), BF16)],
        compiler_params=_params(("parallel", "parallel")),
        name="sb_proj",
    )(x3, g.reshape(1, d), wqt, wk, wvt)

    rj = jnp.arange(kb + SUBLANES)
    sk = jnp.arange(2 * kb) % kb
    tri = jnp.where((rj[:, None] >= kb) | (sk[None, :] > rj[:, None]), 1.0, 0.0).astype(BF16)
    pair = 2 * dh
    o = pl.pallas_call(
        _sb_attn_kernel,
        grid=(b, hh // 2, nb),
        in_specs=[pl.BlockSpec((1, 1, pair, t), lambda bi, h, i: (bi, i, h, 0)),
                  pl.BlockSpec((1, s, pair), lambda bi, h, i: (bi, 0, h)),
                  pl.BlockSpec((1, s // sweep, pair, sweep), lambda bi, h, i: (bi, 0, h, 0)),
                  _const_spec((kb + SUBLANES, 2 * kb))],
        out_specs=pl.BlockSpec((1, t, pair), lambda bi, h, i: (bi, i, h)),
        out_shape=jax.ShapeDtypeStruct((b, s, hh * dh), BF16),
        scratch_shapes=[pltpu.VMEM((2, SUBLANES, t), F32), pltpu.VMEM((2, dh, t), F32)],
        compiler_params=_params(("parallel", "parallel", "arbitrary")),
        name="sb_attn",
    )(qt, k, vt, tri)
    return _out_proj(x3.reshape(b * s, d), o.reshape(b * s, hh * dh), w_out).reshape(b, s, d)


MLA_QK_PAD = MLA_NOPE + 2 * MLA_ROPE


def _mla_proj_kernel(x_ref, g_ref, waq_ref, wakv_ref, war_ref, gqa_ref, gkva_ref, wuqt_ref,
                     wuk_ref, wuvt_ref, gmat_ref, gq_ref, cst_ref, gkn_ref, gkr_ref, csk_ref,
                     qt_ref, k_ref, vt_ref):
    nope, rope, qp = MLA_NOPE, MLA_ROPE, MLA_QK_PAD
    h = _rms_bf16(x_ref[0], g_ref[...])
    cq = _rms_bf16(jnp.dot(h, waq_ref[...], preferred_element_type=F32), gqa_ref[...])
    ckv = _rms_bf16(jnp.dot(h, wakv_ref[...], preferred_element_type=F32), gkva_ref[...])
    yqt = _dot_nt(wuqt_ref[...], cq)
    cs = cst_ref[...]
    for hd in range(MLA_HEADS):
        base = hd * qp
        y = yqt[base:base + qp, :]
        gcol = gq_ref[base:base + qp, :]
        qt_ref[0, 0, base:base + nope, :] = _rows_rms(y[:nope], gcol[:nope]).astype(BF16)
        msr = jnp.mean(y[nope:nope + rope] * y[nope:nope + rope], axis=0, keepdims=True)
        rr = y[nope:] * lax.rsqrt(msr + RMS_EPS) * gcol[nope:] * cs
        qf = (rr[:rope] + rr[rope:]).astype(BF16)
        qt_ref[0, 0, base + nope:base + nope + rope, :] = qf
        qt_ref[0, 0, base + nope + rope:base + qp, :] = qf
    width = gmat_ref.shape[0]
    gm = gmat_ref[...]
    sh = jnp.dot(h, war_ref[...], preferred_element_type=F32)
    ms = _split_dot(sh * sh, gm[:qp, :qp])
    shn = sh * lax.rsqrt(ms + RMS_EPS) * gkr_ref[...] * csk_ref[...]
    shared = jnp.concatenate([shn] * (width // qp), axis=1)
    ykn = jnp.dot(ckv, wuk_ref[...], preferred_element_type=F32)
    for c in range(ykn.shape[1] // width):
        sl = slice(c * width, (c + 1) * width)
        y = ykn[:, sl]
        ms = _split_dot(y * y, gm)
        k_ref[0, :, sl] = (y * lax.rsqrt(ms + RMS_EPS) * gkn_ref[:, sl] + shared).astype(BF16)
    vt_ref[0, 0] = _dot_nt(wuvt_ref[...], ckv).astype(BF16)


def _mla_attn_kernel(qt_ref, k_ref, vt_ref, o_ref, m_ref, l_ref, acc_ref, s_ref, mb_ref):
    t = qt_ref.shape[3]
    qp, dv = MLA_QK_PAD, MLA_V
    i = pl.program_id(2)
    m_ref[...] = jnp.full(m_ref.shape, -jnp.inf, F32)
    l_ref[...] = jnp.zeros(l_ref.shape, F32)
    acc_ref[...] = jnp.zeros(acc_ref.shape, F32)

    def logits(j, kind, slot):
        k = k_ref[0, pl.ds(pl.multiple_of(j * t, t), t), :]
        ss = [jnp.dot(k[:, a * qp:(a + 1) * qp], qt_ref[0, 0, a * qp:(a + 1) * qp, :],
                      preferred_element_type=F32) for a in range(2)]
        for a in range(2):
            s = ss[a]
            if kind == DIAG:
                keys = lax.broadcasted_iota(jnp.int32, (t, t), 0)
                queries = lax.broadcasted_iota(jnp.int32, (t, t), 1)
                s = jnp.where((keys // CHUNK) <= (queries // CHUNK), s, MASK_VALUE)
            _stage_logits(s, s_ref.at[slot, a], mb_ref.at[slot, a])

    def absorb(j, slot):
        vt = vt_ref[0, j]
        for a in range(2):
            _softmax_step_t(s_ref.at[slot, a], mb_ref.at[slot, a], vt[a * dv:(a + 1) * dv, :],
                            m_ref.at[a], l_ref.at[a], acc_ref.at[a])

    _sweep_key_tiles(i, logits, absorb)
    o = jnp.concatenate([acc_ref[0] / l_ref[0], acc_ref[1] / l_ref[1]], axis=0)
    o_ref[0] = o.T.astype(BF16)


def _rotate_half_cols(w):
    half = MLA_ROPE // 2
    return jnp.concatenate([-w[..., half:], w[..., :half]], axis=-1)


def _mla_mixer(x3, g, w_a, q_a_norm, kv_a_norm, w_uq, w_ukv, q_norm_g, k_norm_g, w_out):
    b, s, d = x3.shape
    t = TOK_TILE
    nb = s // t
    hh = MLA_HEADS
    nope, rope, qp, dv = MLA_NOPE, MLA_ROPE, MLA_QK_PAD, MLA_V
    half = rope // 2
    wa_q = w_a[:, :MLA_Q_LORA].astype(BF16)
    wa_kv = w_a[:, MLA_Q_LORA:MLA_Q_LORA + MLA_KV_LORA].astype(BF16)
    wa_r = w_a[:, MLA_Q_LORA + MLA_KV_LORA:]
    war = jnp.concatenate([jnp.zeros((d, nope), F32), wa_r, _rotate_half_cols(wa_r)], axis=1).astype(BF16)
    wq3 = w_uq.reshape(MLA_Q_LORA, hh, nope + rope)
    wuqt = jnp.concatenate([wq3, _rotate_half_cols(wq3[:, :, nope:])], axis=2
                           ).reshape(MLA_Q_LORA, hh * qp).T.astype(BF16)
    wkv3 = w_ukv.reshape(MLA_KV_LORA, hh, nope + dv)
    wuk = jnp.concatenate([wkv3[:, :, :nope], jnp.zeros((MLA_KV_LORA, hh, 2 * rope), F32)], axis=2
                          ).reshape(MLA_KV_LORA, hh * qp).astype(BF16)
    wuvt = wkv3[:, :, nope:].reshape(MLA_KV_LORA, hh * dv).T.astype(BF16)
    swap = lambda v: jnp.concatenate([v[half:], v[:half]])
    scale = (nope + rope) ** -0.5
    gq_head = jnp.concatenate([q_norm_g, swap(q_norm_g[nope:])]) * (scale * LOG2E)
    gq = jnp.tile(gq_head, hh).reshape(hh * qp, 1)
    gkn = jnp.tile(jnp.concatenate([k_norm_g[:nope], jnp.zeros((2 * rope,), F32)]), hh).reshape(1, hh * qp)
    gkr = jnp.concatenate([jnp.zeros((nope,), F32), k_norm_g[nope:], swap(k_norm_g[nope:])]).reshape(1, qp)
    inv = ROPE_BASE ** (-jnp.arange(0, rope, 2, dtype=F32) / rope)
    ang = jnp.arange(s, dtype=F32)[:, None] * inv[None, :]
    cos, sin = jnp.cos(ang), jnp.sin(ang)
    cs = jnp.concatenate([cos, cos, sin, sin], axis=1)
    csk = jnp.concatenate([jnp.zeros((s, nope), F32), cs], axis=1)
    gmat = _group_mean_matrix([nope, rope, rope], 2 * qp)
    feat_major = lambda w: pl.BlockSpec((1, 1, w, t), lambda bi, i: (bi, i, 0, 0))
    qt, k, vt = pl.pallas_call(
        _mla_proj_kernel,
        grid=(b, nb),
        in_specs=[pl.BlockSpec((1, t, d), lambda bi, i: (bi, i, 0)),
                  _const_spec((1, d)),
                  _const_spec((d, MLA_Q_LORA)),
                  _const_spec((d, MLA_KV_LORA)),
                  _const_spec((d, qp)),
                  _const_spec((1, MLA_Q_LORA)),
                  _const_spec((1, MLA_KV_LORA)),
                  _const_spec((hh * qp, MLA_Q_LORA)),
                  _const_spec((MLA_KV_LORA, hh * qp)),
                  _const_spec((hh * dv, MLA_KV_LORA)),
                  _const_spec((2 * qp, 2 * qp)),
                  _const_spec((hh * qp, 1)),
                  pl.BlockSpec((2 * rope, t), lambda bi, i: (0, i)),
                  _const_spec((1, hh * qp)),
                  _const_spec((1, qp)),
                  pl.BlockSpec((t, qp), lambda bi, i: (i, 0))],
        out_specs=[feat_major(hh * qp),
                   pl.BlockSpec((1, t, hh * qp), lambda bi, i: (bi, i, 0)),
                   feat_major(hh * dv)],
        out_shape=[jax.ShapeDtypeStruct((b, nb, hh * qp, t), BF16),
                   jax.ShapeDtypeStruct((b, s, hh * qp), BF16),
                   jax.ShapeDtypeStruct((b, nb, hh * dv, t), BF16)],
        compiler_params=_params(("parallel", "parallel")),
        name="mla_proj",
    )(x3, g.reshape(1, d), wa_q, wa_kv, war, q_a_norm.reshape(1, -1), kv_a_norm.reshape(1, -1),
      wuqt, wuk, wuvt, gmat, gq, cs.T, gkn, gkr, csk)

    o = pl.pallas_call(
        _mla_attn_kernel,
        grid=(b, hh // 2, nb),
        in_specs=[pl.BlockSpec((1, 1, 2 * qp, t), lambda bi, h, i: (bi, i, h, 0)),
                  pl.BlockSpec((1, s, 2 * qp), lambda bi, h, i: (bi, 0, h)),
                  pl.BlockSpec((1, nb, 2 * dv, t), lambda bi, h, i: (bi, 0, h, 0))],
        out_specs=pl.BlockSpec((1, t, 2 * dv), lambda bi, h, i: (bi, i, h)),
        out_shape=jax.ShapeDtypeStruct((b, s, hh * dv), BF16),
        scratch_shapes=[pltpu.VMEM((2, 1, t), F32), pltpu.VMEM((2, 1, t), F32),
                        pltpu.VMEM((2, dv, t), F32),
                        pltpu.VMEM((2, 2, t, t), F32),
                        pltpu.VMEM((2, 2, 1, t), F32)],
        compiler_params=_params(("parallel", "parallel", "arbitrary")),
        name="mla_attn",
    )(qt, k, vt)
    return _out_proj(x3.reshape(b * s, d), o.reshape(b * s, hh * dv), w_out).reshape(b, s, d)


def kernel(x, rel_bias_table, ffn_norm, ffn_w_in, ffn_w_out, mixer_norm, conv_w_in, conv_b_in, conv_w_dw, conv_b_dw, conv_ln_g, conv_ln_b, conv_w_out, conv_b_out, diff_w_in, diff_q_norm, diff_k_norm, diff_lambda, diff_sub_norm, diff_w_out, sb_w_in, sb_w_out, mla_w_a, mla_q_a_norm, mla_kv_a_norm, mla_w_uq, mla_w_ukv, mla_q_norm, mla_k_norm, mla_w_out):
    b, s, d = x.shape
    assert d == D_MODEL and s % TOK_TILE == 0
    depth = ffn_norm.shape[0]
    ffn = lambda xx, i, k: _ffn(xx.reshape(b * s, d), ffn_norm[i, k], ffn_w_in[i, k],
                                ffn_w_out[i, k]).reshape(b, s, d)
    for i in range(depth):
        mixer, j = i % N_MIXERS, i // N_MIXERS
        x = ffn(x, i, 0)
        g = mixer_norm[i]
        if mixer == 0:
            x = _conv_mixer(x, g, conv_w_in[j], conv_b_in[j], conv_w_dw[j], conv_b_dw[j],
                            conv_ln_g[j], conv_ln_b[j], conv_w_out[j], conv_b_out[j])
        elif mixer == 1:
            x = _diff_mixer(x, i, g, diff_w_in[j], diff_q_norm[j], diff_k_norm[j], diff_lambda[j],
                            diff_sub_norm[j], diff_w_out[j], rel_bias_table)
        elif mixer == 2:
            x = _sb_mixer(x, g, sb_w_in[j], sb_w_out[j])
        else:
            x = _mla_mixer(x, g, mla_w_a[j], mla_q_a_norm[j], mla_kv_a_norm[j], mla_w_uq[j],
                           mla_w_ukv[j], mla_q_norm[j], mla_k_norm[j], mla_w_out[j])
        x = ffn(x, i, 1)
    return x
```

```python
import functools
import math

import jax
import jax.numpy as jnp
from jax import lax
from jax.experimental import pallas as pl
from jax.experimental.pallas import tpu as pltpu

F32 = jnp.float32
BF16 = jnp.bfloat16

D_MODEL = 1024
DEPTH = 4
CHUNK = 64
N_MIXERS = 4
D_FF = 2816
RMS_EPS = 1e-6
LN_EPS = 1e-5
MASK_VALUE = -1e30
CONV_WIDTH = 31
DIFF_HEADS = 8
DIFF_QK_DIM = 64
DIFF_V_DIM = 128
LAMBDA_INIT_BASE = 0.8
LAMBDA_INIT_SCALE = 0.6
LAMBDA_INIT_DECAY = 0.3
REL_BUCKETS = 32
REL_MAX_DIST = 128
SB_HEADS = 16
SB_HEAD_DIM = 64
MLA_HEADS = 16
MLA_Q_LORA = 384
MLA_KV_LORA = 256
MLA_NOPE = 64
MLA_ROPE = 32
MLA_V = 64
ROPE_BASE = 10000.0
LOG2E = math.log2(math.e)

LANES = 128
SUBLANES = 8
TOK_TILE = 512
FF_CHUNK = 256
FAR_UNROLL = 4
SB_KEY_BLOCK = 128
SB_SWEEP_KEYS = 256
CONV_HALO = 32
CONV_ROWS = 32
VMEM_LIMIT = 56 * 1024 * 1024
F32_EXP2_ZERO = -151.0


def _params(sem):
    return pltpu.CompilerParams(dimension_semantics=sem, vmem_limit_bytes=VMEM_LIMIT)


def _const_spec(shape):
    nd = len(shape)
    return pl.BlockSpec(shape, lambda *_: (0,) * nd, pipeline_mode=pl.Buffered(1))


def _rms_bf16(x, g):
    ms = jnp.mean(x * x, axis=-1, keepdims=True)
    return (x * lax.rsqrt(ms + RMS_EPS) * g).astype(BF16)


def _split_dot(x, w):
    hi = x.astype(BF16)
    lo = (x - hi.astype(F32)).astype(BF16)
    return (jnp.dot(hi, w, preferred_element_type=F32)
            + jnp.dot(lo, w, preferred_element_type=F32))


def _dot_nt(a, b):
    return lax.dot_general(a, b, (((1,), (1,)), ((), ())), preferred_element_type=F32)


def _rows_rms(y, g_col):
    ms = jnp.mean(y * y, axis=0, keepdims=True)
    return y * lax.rsqrt(ms + RMS_EPS) * g_col


def _ffn_kernel(*refs, fused_proj):
    if fused_proj:
        x_ref, a_ref, wa_ref, g_ref, win_ref, wout_ref, o_ref, gate_ref = refs
        x = x_ref[...] + jnp.dot(a_ref[...], wa_ref[...], preferred_element_type=F32)
    else:
        x_ref, g_ref, win_ref, wout_ref, o_ref, gate_ref = refs
        x = x_ref[...]
    h = _rms_bf16(x, g_ref[...])
    for c in range(D_FF // FF_CHUNK):
        lo, hi = c * FF_CHUNK, (c + 1) * FF_CHUNK
        a = jnp.dot(h, win_ref[:, lo:hi].astype(BF16), preferred_element_type=F32)
        u = jnp.dot(h, win_ref[:, D_FF + lo:D_FF + hi].astype(BF16), preferred_element_type=F32)
        gate_ref[:, lo:hi] = (a * jax.nn.sigmoid(a) * u).astype(BF16)
    y = jnp.dot(gate_ref[...], wout_ref[...].astype(BF16), preferred_element_type=F32)
    o_ref[...] = x + 0.5 * y


def _ffn(x2, g, w_in_all, w_out_all, layer, half, proj=None):
    n, d = x2.shape
    tm = TOK_TILE
    rows = lambda w: pl.BlockSpec((tm, w), lambda i: (i, 0))
    pick = lambda r, c: pl.BlockSpec((None, None, r, c), lambda i: (layer, half, 0, 0),
                                     pipeline_mode=pl.Buffered(1))
    in_specs, args = [rows(d)], [x2]
    if proj is not None:
        a, w_a = proj
        in_specs += [rows(a.shape[1]), _const_spec(w_a.shape)]
        args += [a, w_a.astype(BF16)]
    in_specs += [_const_spec((1, d)), pick(d, 2 * D_FF), pick(D_FF, d)]
    args += [g.reshape(1, d), w_in_all, w_out_all]
    return pl.pallas_call(
        functools.partial(_ffn_kernel, fused_proj=proj is not None),
        grid=(n // tm,),
        in_specs=in_specs,
        out_specs=rows(d),
        out_shape=jax.ShapeDtypeStruct((n, d), F32),
        scratch_shapes=[pltpu.VMEM((tm, D_FF), BF16)],
        compiler_params=_params(("parallel",)),
        name="ffn_proj" if proj is not None else "ffn",
    )(*args)


def _conv_in_kernel(x_ref, g_ref, w_ref, b_ref, u_ref):
    d = D_MODEL
    h = _rms_bf16(x_ref[...], g_ref[...])
    y = jnp.dot(h, w_ref[...], preferred_element_type=F32) + b_ref[...]
    u_ref[...] = y[:, :d] * jax.nn.sigmoid(y[:, d:])


def _conv_out_kernel(x_ref, ucur_ref, uprev_ref, wdw_ref, bdw_ref, lng_ref, lnb_ref,
                     wout_ref, bout_ref, y_ref, ext_ref, conv_ref):
    ts = ucur_ref.shape[1]
    i = pl.program_id(1)
    ext_ref[0, 0:CONV_HALO, :] = jnp.where(i > 0, uprev_ref[0], 0.0)
    ext_ref[0, CONV_HALO:, :] = ucur_ref[0]
    n_shift = ts + CONV_HALO - SUBLANES
    for p in range(1, SUBLANES):
        ext_ref[p, 0:n_shift, :] = ext_ref[0, p:p + n_shift, :]
    off = CONV_HALO - (CONV_WIDTH - 1)
    for r in range(ts // CONV_ROWS):
        r0 = r * CONV_ROWS
        acc = None
        for k in range(CONV_WIDTH):
            p = (off + k) % SUBLANES
            a = r0 + off + k - p
            tap = ext_ref[p, a:a + CONV_ROWS, :] * wdw_ref[k:k + 1, :]
            acc = tap if acc is None else acc + tap
        conv_ref[r0:r0 + CONV_ROWS, :] = acc + bdw_ref[...]
    c = conv_ref[...]
    mu = jnp.mean(c, axis=-1, keepdims=True)
    cc = c - mu
    var = jnp.mean(cc * cc, axis=-1, keepdims=True)
    ln = cc * lax.rsqrt(var + LN_EPS) * lng_ref[...] + lnb_ref[...]
    act = (ln * jax.nn.sigmoid(ln)).astype(BF16)
    y = jnp.dot(act, wout_ref[...], preferred_element_type=F32) + bout_ref[...]
    y_ref[0] = x_ref[0] + y


def _conv_mixer(x3, g, w_in, b_in, w_dw, b_dw, ln_g, ln_b, w_out, b_out):
    b, s, d = x3.shape
    n = b * s
    tm = TOK_TILE
    u = pl.pallas_call(
        _conv_in_kernel,
        grid=(n // tm,),
        in_specs=[pl.BlockSpec((tm, d), lambda i: (i, 0)),
                  _const_spec((1, d)),
                  _const_spec((d, 2 * d)),
                  _const_spec((1, 2 * d))],
        out_specs=pl.BlockSpec((tm, d), lambda i: (i, 0)),
        out_shape=jax.ShapeDtypeStruct((n, d), F32),
        compiler_params=_params(("parallel",)),
        name="conv_in",
    )(x3.reshape(n, d), g.reshape(1, d), w_in.astype(BF16), b_in.reshape(1, 2 * d))
    u3 = u.reshape(b, s, d)
    halo_per_tile = tm // CONV_HALO
    row = lambda v: v.reshape(1, d)
    return pl.pallas_call(
        _conv_out_kernel,
        grid=(b, s // tm),
        in_specs=[pl.BlockSpec((1, tm, d), lambda bi, i: (bi, i, 0)),
                  pl.BlockSpec((1, tm, d), lambda bi, i: (bi, i, 0)),
                  pl.BlockSpec((1, CONV_HALO, d),
                               lambda bi, i: (bi, jnp.maximum(i * halo_per_tile - 1, 0), 0)),
                  _const_spec((CONV_WIDTH, d)),
                  _const_spec((1, d)), _const_spec((1, d)), _const_spec((1, d)),
                  _const_spec((d, d)),
                  _const_spec((1, d))],
        out_specs=pl.BlockSpec((1, tm, d), lambda bi, i: (bi, i, 0)),
        out_shape=jax.ShapeDtypeStruct((b, s, d), F32),
        scratch_shapes=[pltpu.VMEM((SUBLANES, tm + CONV_HALO, d), F32), pltpu.VMEM((tm, d), F32)],
        compiler_params=_params(("parallel", "parallel")),
        name="conv_out",
    )(x3, u3, u3, w_dw, row(b_dw), row(ln_g), row(ln_b), w_out.astype(BF16), row(b_out))


def _stage_logits(s, s_ref, mb_ref):
    s_ref[...] = s
    mb_ref[...] = jnp.max(s, axis=0, keepdims=True)


def _softmax_step_t(s_ref, mb_ref, vt, m_ref, l_ref, acc_ref):
    m_prev = m_ref[...]
    m_new = jnp.maximum(m_prev, mb_ref[...])
    alpha = jnp.exp2(m_prev - m_new)
    p = jnp.exp2(s_ref[...] - m_new)
    l_ref[...] = alpha * l_ref[...] + jnp.sum(p, axis=0, keepdims=True)
    acc_ref[...] = alpha * acc_ref[...] + jnp.dot(vt, p.astype(BF16), preferred_element_type=F32)
    m_ref[...] = m_new


FAR, PREV, DIAG = "far", "prev", "diag"


def _sweep_key_tiles(i, logits, absorb):
    assert FAR_UNROLL == 4, "the remainder below is handled as one pair plus one single tile"
    n_far = jnp.maximum(i - 1, 0)
    n_groups = n_far // FAR_UNROLL

    def far_group(jg, carry):
        base = jg * FAR_UNROLL
        logits(base, FAR, 0)
        for u in range(FAR_UNROLL):
            if u + 1 < FAR_UNROLL:
                logits(base + u + 1, FAR, (u + 1) % 2)
            absorb(base + u, u % 2)
        return carry

    lax.fori_loop(0, n_groups, far_group, 0)
    rest = n_groups * FAR_UNROLL

    @pl.when(n_far - rest >= 2)
    def _far_pair():
        logits(rest, FAR, 0)
        logits(rest + 1, FAR, 1)
        absorb(rest, 0)
        absorb(rest + 1, 1)

    @pl.when((n_far - rest) % 2 == 1)
    def _far_single():
        logits(n_far - 1, FAR, 0)
        absorb(n_far - 1, 0)

    @pl.when(i >= 1)
    def _last_two():
        logits(i - 1, PREV, 0)
        logits(i, DIAG, 1)
        absorb(i - 1, 0)
        absorb(i, 1)

    @pl.when(i == 0)
    def _only_diag():
        logits(i, DIAG, 0)
        absorb(i, 0)


def _group_mean_matrix(groups, width):
    idx = []
    for gi, size in enumerate(groups):
        idx += [gi] * size
    reps = width // len(idx)
    gid = jnp.asarray([r * len(groups) + g for r in range(reps) for g in idx], jnp.int32)
    sizes = jnp.asarray([float(groups[g]) for _ in range(reps) for g in idx], F32)
    same = gid[:, None] == gid[None, :]
    return jnp.where(same, 1.0 / sizes[None, :], 0.0).astype(BF16)


def _diff_proj_kernel(x_ref, g_ref, wqt_ref, wk_ref, wvt_ref, gq_ref, gk_ref, gmat_ref,
                      qt_ref, k_ref, vt_ref):
    h = _rms_bf16(x_ref[0], g_ref[...])
    yqt = _dot_nt(wqt_ref[...], h)
    for r in range(yqt.shape[0] // DIFF_QK_DIM):
        sl = slice(r * DIFF_QK_DIM, (r + 1) * DIFF_QK_DIM)
        qt_ref[0, 0, sl, :] = _rows_rms(yqt[sl, :], gq_ref[sl, :]).astype(BF16)
    yk = jnp.dot(h, wk_ref[...], preferred_element_type=F32)
    width = gmat_ref.shape[0]
    for c in range(yk.shape[1] // width):
        sl = slice(c * width, (c + 1) * width)
        y = yk[:, sl]
        ms = _split_dot(y * y, gmat_ref[...])
        k_ref[0, :, sl] = (y * lax.rsqrt(ms + RMS_EPS) * gk_ref[:, sl]).astype(BF16)
    vt_ref[0, 0] = _dot_nt(wvt_ref[...], h).astype(BF16)


def _diff_attn_kernel(tab_ref, scal_ref, qt_ref, k_ref, vt_ref, bucket_ref, subg_ref, o_ref,
                      bias_ref, m_ref, l_ref, acc_ref, s_ref, mb_ref):
    t = qt_ref.shape[3]
    hd = pl.program_id(0)
    i = pl.program_id(2)

    @pl.when(jnp.logical_and(pl.program_id(1) == 0, i == 0))
    def _build_bias():
        far = tab_ref[REL_BUCKETS // 2 - 1, hd]
        keys = lax.broadcasted_iota(jnp.int32, (t, t), 0)
        queries = lax.broadcasted_iota(jnp.int32, (t, t), 1)
        visible = (keys // CHUNK) <= (queries // CHUNK)
        for tile in range(2):
            bk = bucket_ref[tile]
            bias = jnp.zeros((t, t), F32)
            for b in range(REL_BUCKETS):
                bias = jnp.where(bk == b, (tab_ref[b, hd] - far) * LOG2E, bias)
            if tile == 1:
                bias = jnp.where(visible, bias, MASK_VALUE)
            bias_ref[tile] = bias

    m_ref[...] = jnp.full(m_ref.shape, -jnp.inf, F32)
    l_ref[...] = jnp.zeros(l_ref.shape, F32)
    acc_ref[...] = jnp.zeros(acc_ref.shape, F32)

    qt = qt_ref[0, 0]
    feat = lax.broadcasted_iota(jnp.int32, qt.shape, 0)
    qmaps = [jnp.where((feat // DIFF_QK_DIM) == mp, qt, jnp.zeros_like(qt)) for mp in range(2)]

    bias_tile = {FAR: None, PREV: 0, DIAG: 1}

    def logits(j, kind, slot):
        k = k_ref[0, pl.ds(pl.multiple_of(j * t, t), t), :]
        ss = [jnp.dot(k, qmaps[mp], preferred_element_type=F32) for mp in range(2)]
        for mp in range(2):
            s = ss[mp] if kind == FAR else ss[mp] + bias_ref[bias_tile[kind]]
            _stage_logits(s, s_ref.at[slot, mp], mb_ref.at[slot, mp])

    def absorb(j, slot):
        vt = vt_ref[0, j]
        for mp in range(2):
            _softmax_step_t(s_ref.at[slot, mp], mb_ref.at[slot, mp], vt,
                            m_ref.at[mp], l_ref.at[mp], acc_ref.at[mp])

    _sweep_key_tiles(i, logits, absorb)

    lam = scal_ref[0]
    attn = acc_ref[0] / l_ref[0] - lam * (acc_ref[1] / l_ref[1])
    o_ref[0] = _rows_rms(attn, subg_ref[...]).T.astype(BF16)


def _t5_bucket(rel):
    nb = REL_BUCKETS // 2
    bucket = jnp.where(rel > 0, nb, 0)
    n = jnp.abs(rel)
    max_exact = nb // 2
    n_f = jnp.maximum(n, 1).astype(jnp.float32)
    large = max_exact + (jnp.log(n_f / max_exact) / math.log(REL_MAX_DIST / max_exact)
                         * (nb - max_exact)).astype(jnp.int32)
    large = jnp.minimum(large, nb - 1)
    return bucket + jnp.where(n < max_exact, n, large)


def _diff_mixer(x3, layer_idx, g, w_in, q_norm_g, k_norm_g, lam, sub_norm_g, w_out, rel_table):
    b, s, d = x3.shape
    t = TOK_TILE
    nb = s // t
    hh, dq, dv = DIFF_HEADS, DIFF_QK_DIM, DIFF_V_DIM
    w3 = w_in.reshape(d, hh, 4 * dq + dv)
    wqt = w3[:, :, :2 * dq].reshape(d, hh * 2 * dq).T.astype(BF16)
    wk = w3[:, :, 2 * dq:4 * dq].reshape(d, hh * 2 * dq).astype(BF16)
    wvt = w3[:, :, 4 * dq:].reshape(d, hh * dv).T.astype(BF16)
    scale = dq ** -0.5
    gq = (jnp.tile(q_norm_g, 2 * hh) * (scale * LOG2E)).reshape(hh * 2 * dq, 1)
    gk = jnp.tile(k_norm_g, 2 * hh).reshape(1, hh * 2 * dq)
    gmat = _group_mean_matrix([dq], 2 * LANES)
    feat_major = lambda w: pl.BlockSpec((1, 1, w, t), lambda bi, i: (bi, i, 0, 0))
    qt, k, vt = pl.pallas_call(
        _diff_proj_kernel,
        grid=(b, nb),
        in_specs=[pl.BlockSpec((1, t, d), lambda bi, i: (bi, i, 0)),
                  _const_spec((1, d)),
                  _const_spec((hh * 2 * dq, d)),
                  _const_spec((d, hh * 2 * dq)),
                  _const_spec((hh * dv, d)),
                  _const_spec((hh * 2 * dq, 1)),
                  _const_spec((1, hh * 2 * dq)),
                  _const_spec((2 * LANES, 2 * LANES))],
        out_specs=[feat_major(hh * 2 * dq),
                   pl.BlockSpec((1, t, hh * 2 * dq), lambda bi, i: (bi, i, 0)),
                   feat_major(hh * dv)],
        out_shape=[jax.ShapeDtypeStruct((b, nb, hh * 2 * dq, t), BF16),
                   jax.ShapeDtypeStruct((b, s, hh * 2 * dq), BF16),
                   jax.ShapeDtypeStruct((b, nb, hh * dv, t), BF16)],
        compiler_params=_params(("parallel", "parallel")),
        name="diff_proj",
    )(x3, g.reshape(1, d), wqt, wk, wvt, gq, gk, gmat)

    lam_init = LAMBDA_INIT_BASE - LAMBDA_INIT_SCALE * math.exp(-LAMBDA_INIT_DECAY * layer_idx)
    lam_full = (jnp.exp(jnp.sum(lam[0] * lam[1]).astype(F32))
                - jnp.exp(jnp.sum(lam[2] * lam[3]).astype(F32)) + lam_init)
    scal = jnp.reshape(lam_full, (1,)).astype(F32)
    r = jnp.arange(t)
    rel_diag = r[:, None] - r[None, :]
    buckets = jnp.stack([_t5_bucket(rel_diag - t), _t5_bucket(rel_diag)]).astype(jnp.int32)
    subg = (sub_norm_g * (1.0 - lam_init)).reshape(dv, 1)
    smem = pl.BlockSpec(memory_space=pltpu.SMEM)
    o = pl.pallas_call(
        _diff_attn_kernel,
        grid=(hh, b, nb),
        in_specs=[smem, smem,
                  pl.BlockSpec((1, 1, 2 * dq, t), lambda h, bi, i: (bi, i, h, 0)),
                  pl.BlockSpec((1, s, 2 * dq), lambda h, bi, i: (bi, 0, h)),
                  pl.BlockSpec((1, nb, dv, t), lambda h, bi, i: (bi, 0, h, 0)),
                  _const_spec((2, t, t)),
                  _const_spec((dv, 1))],
        out_specs=pl.BlockSpec((1, t, dv), lambda h, bi, i: (bi, i, h)),
        out_shape=jax.ShapeDtypeStruct((b, s, hh * dv), BF16),
        scratch_shapes=[pltpu.VMEM((2, t, t), F32),
                        pltpu.VMEM((2, 1, t), F32),
                        pltpu.VMEM((2, 1, t), F32),
                        pltpu.VMEM((2, dv, t), F32),
                        pltpu.VMEM((2, 2, t, t), F32),
                        pltpu.VMEM((2, 2, 1, t), F32)],
        compiler_params=_params(("parallel", "arbitrary", "arbitrary")),
        name="diff_attn",
    )(rel_table.astype(F32), scal, qt, k, vt, buckets, subg)
    return o.reshape(b * s, hh * dv), w_out


def _sb_proj_kernel(x_ref, g_ref, wqt_ref, wk_ref, wvt_ref, qt_ref, k_ref, vt_ref):
    h = _rms_bf16(x_ref[0], g_ref[...])
    qt_ref[0, 0] = (_dot_nt(wqt_ref[...], h) * (SB_HEAD_DIM ** -0.5 * LOG2E)).astype(BF16)
    k_ref[0] = jnp.dot(h, wk_ref[...], preferred_element_type=F32).astype(BF16)
    yvt = _dot_nt(wvt_ref[...], h).astype(BF16)
    sk = vt_ref.shape[3]
    for c in range(vt_ref.shape[1]):
        vt_ref[0, c] = yvt[:, c * sk:(c + 1) * sk]


def _sb_attn_kernel(qt_ref, k_ref, vt_ref, tri_ref, o_ref, run_ref, acc_ref):
    t = qt_ref.shape[3]
    kb = SB_KEY_BLOCK
    sk = SB_SWEEP_KEYS
    dh = SB_HEAD_DIM
    i = pl.program_id(2)
    sub = sk // kb
    qt = qt_ref[0, 0]
    feat = lax.broadcasted_iota(jnp.int32, qt.shape, 0)
    qh = [jnp.where((feat // dh) == a, qt, jnp.zeros_like(qt)) for a in range(2)]
    run_ref[...] = jnp.zeros(run_ref.shape, F32)
    acc_ref[...] = jnp.zeros(acc_ref.shape, F32)

    def logits(j, c0):
        k = k_ref[0, pl.ds(pl.multiple_of(j * sk, sk), sk), :]
        return [jnp.dot(k, qh[a][:, c0:], preferred_element_type=F32) for a in range(2)]

    def fold(j, zs, diagonal, c0):
        qs = slice(c0, t)
        vt = vt_ref[0, j]
        if diagonal:
            causal = (lax.broadcasted_iota(jnp.int32, zs[0].shape, 0)
                      < lax.broadcasted_iota(jnp.int32, zs[0].shape, 1))
        for a in range(2):
            z = zs[a]
            soft = jnp.log2(1.0 + jnp.exp2(-jnp.abs(z)))
            log_beta = jnp.minimum(z, 0.0) - soft
            log_keep = log_beta - z
            if diagonal:
                log_keep = jnp.where(causal, log_keep, 0.0)
            hi = log_keep.astype(BF16)
            lo = (log_keep - hi.astype(F32)).astype(BF16)
            run = run_ref[a, :, qs]
            ws = [None] * sub
            for blk in range(sub - 1, -1, -1):
                rows = slice(blk * kb, (blk + 1) * kb)
                sums = jnp.dot(tri_ref[...], jnp.concatenate([hi[rows], lo[rows]], axis=0),
                               preferred_element_type=F32)
                ws[blk] = jnp.exp2(log_beta[rows] + sums[:kb] + run[0:1])
                run = run + sums[kb:]
            w = jnp.concatenate(ws, axis=0)
            if diagonal:
                w = jnp.where(causal, w, 0.0)
            acc_ref[a, :, qs] += jnp.dot(vt[a * dh:(a + 1) * dh, :], w.astype(BF16),
                                         preferred_element_type=F32)
            run_ref[a, :, qs] = run

    per_tile = t // sk
    own = [(i * per_tile + b, True, b * sk) for b in range(per_tile - 1, -1, -1)]

    def run_blocks(blocks):
        zs = logits(blocks[0][0], blocks[0][2])
        for n, (j, diagonal, c0) in enumerate(blocks):
            nxt = logits(blocks[n + 1][0], blocks[n + 1][2]) if n + 1 < len(blocks) else None
            fold(j, zs, diagonal, c0)
            zs = nxt

    @pl.when(i >= 1)
    def _with_previous():
        run_blocks(own + [(i * per_tile - 1, False, 0)])

    @pl.when(i == 0)
    def _first_tile():
        run_blocks(own)

    def cond(c):
        j, top = c
        return jnp.logical_and(j >= 0, top > F32_EXP2_ZERO)

    def body(c):
        j, _ = c
        fold(j, logits(j, 0), False, 0)
        return j - 1, jnp.max(run_ref[...])

    lax.while_loop(cond, body, (i * per_tile - 2, jnp.max(run_ref[...])))
    o_ref[0] = jnp.concatenate([acc_ref[0], acc_ref[1]], axis=0).T.astype(BF16)


def _sb_mixer(x3, g, w_in, w_out):
    b, s, d = x3.shape
    t = TOK_TILE
    kb = SB_KEY_BLOCK
    sweep = SB_SWEEP_KEYS
    nb = s // t
    hh, dh = SB_HEADS, SB_HEAD_DIM
    w3 = w_in.reshape(d, hh, 3 * dh)
    wqt = w3[:, :, :dh].reshape(d, hh * dh).T.astype(BF16)
    wk = w3[:, :, dh:2 * dh].reshape(d, hh * dh).astype(BF16)
    wvt = w3[:, :, 2 * dh:].reshape(d, hh * dh).T.astype(BF16)
    qt, k, vt = pl.pallas_call(
        _sb_proj_kernel,
        grid=(b, nb),
        in_specs=[pl.BlockSpec((1, t, d), lambda bi, i: (bi, i, 0)),
                  _const_spec((1, d)),
                  _const_spec((hh * dh, d)),
                  _const_spec((d, hh * dh)),
                  _const_spec((hh * dh, d))],
        out_specs=[pl.BlockSpec((1, 1, hh * dh, t), lambda bi, i: (bi, i, 0, 0)),
                   pl.BlockSpec((1, t, hh * dh), lambda bi, i: (bi, i, 0)),
                   pl.BlockSpec((1, t // sweep, hh * dh, sweep), lambda bi, i: (bi, i, 0, 0))],
        out_shape=[jax.ShapeDtypeStruct((b, nb, hh * dh, t), BF16),
                   jax.ShapeDtypeStruct((b, s, hh * dh), BF16),
                   jax.ShapeDtypeStruct((b, s // sweep, hh * dh, sweep), BF16)],
        compiler_params=_params(("parallel", "parallel")),
        name="sb_proj",
    )(x3, g.reshape(1, d), wqt, wk, wvt)

    rj = jnp.arange(kb + SUBLANES)
    sk = jnp.arange(2 * kb) % kb
    tri = jnp.where((rj[:, None] >= kb) | (sk[None, :] > rj[:, None]), 1.0, 0.0).astype(BF16)
    pair = 2 * dh
    o = pl.pallas_call(
        _sb_attn_kernel,
        grid=(b, hh // 2, nb),
        in_specs=[pl.BlockSpec((1, 1, pair, t), lambda bi, h, i: (bi, i, h, 0)),
                  pl.BlockSpec((1, s, pair), lambda bi, h, i: (bi, 0, h)),
                  pl.BlockSpec((1, s // sweep, pair, sweep), lambda bi, h, i: (bi, 0, h, 0)),
                  _const_spec((kb + SUBLANES, 2 * kb))],
        out_specs=pl.BlockSpec((1, t, pair), lambda bi, h, i: (bi, i, h)),
        out_shape=jax.ShapeDtypeStruct((b, s, hh * dh), BF16),
        scratch_shapes=[pltpu.VMEM((2, SUBLANES, t), F32), pltpu.VMEM((2, dh, t), F32)],
        compiler_params=_params(("parallel", "parallel", "arbitrary")),
        name="sb_attn",
    )(qt, k, vt, tri)
    return o.reshape(b * s, hh * dh), w_out


MLA_QK_PAD = MLA_NOPE + 2 * MLA_ROPE


def _mla_proj_kernel(x_ref, g_ref, waq_ref, wakv_ref, war_ref, gqa_ref, gkva_ref, wuqt_ref,
                     wuk_ref, wuvt_ref, gmat_ref, gq_ref, cst_ref, gkn_ref, gkr_ref, csk_ref,
                     qt_ref, k_ref, vt_ref):
    nope, rope, qp = MLA_NOPE, MLA_ROPE, MLA_QK_PAD
    h = _rms_bf16(x_ref[0], g_ref[...])
    cq = _rms_bf16(jnp.dot(h, waq_ref[...], preferred_element_type=F32), gqa_ref[...])
    ckv = _rms_bf16(jnp.dot(h, wakv_ref[...], preferred_element_type=F32), gkva_ref[...])
    yqt = _dot_nt(wuqt_ref[...], cq)
    cs = cst_ref[...]
    for hd in range(MLA_HEADS):
        base = hd * qp
        y = yqt[base:base + qp, :]
        gcol = gq_ref[base:base + qp, :]
        qt_ref[0, 0, base:base + nope, :] = _rows_rms(y[:nope], gcol[:nope]).astype(BF16)
        msr = jnp.mean(y[nope:nope + rope] * y[nope:nope + rope], axis=0, keepdims=True)
        rr = y[nope:] * lax.rsqrt(msr + RMS_EPS) * gcol[nope:] * cs
        qf = (rr[:rope] + rr[rope:]).astype(BF16)
        qt_ref[0, 0, base + nope:base + nope + rope, :] = qf
        qt_ref[0, 0, base + nope + rope:base + qp, :] = qf
    width = gmat_ref.shape[0]
    gm = gmat_ref[...]
    sh = jnp.dot(h, war_ref[...], preferred_element_type=F32)
    ms = _split_dot(sh * sh, gm[:qp, :qp])
    shn = sh * lax.rsqrt(ms + RMS_EPS) * gkr_ref[...] * csk_ref[...]
    shared = jnp.concatenate([shn] * (width // qp), axis=1)
    ykn = jnp.dot(ckv, wuk_ref[...], preferred_element_type=F32)
    for c in range(ykn.shape[1] // width):
        sl = slice(c * width, (c + 1) * width)
        y = ykn[:, sl]
        ms = _split_dot(y * y, gm)
        k_ref[0, :, sl] = (y * lax.rsqrt(ms + RMS_EPS) * gkn_ref[:, sl] + shared).astype(BF16)
    vt_ref[0, 0] = _dot_nt(wuvt_ref[...], ckv).astype(BF16)


def _mla_attn_kernel(qt_ref, k_ref, vt_ref, o_ref, m_ref, l_ref, acc_ref, s_ref, mb_ref):
    t = qt_ref.shape[3]
    qp, dv = MLA_QK_PAD, MLA_V
    i = pl.program_id(2)
    m_ref[...] = jnp.full(m_ref.shape, -jnp.inf, F32)
    l_ref[...] = jnp.zeros(l_ref.shape, F32)
    acc_ref[...] = jnp.zeros(acc_ref.shape, F32)

    def logits(j, kind, slot):
        k = k_ref[0, pl.ds(pl.multiple_of(j * t, t), t), :]
        ss = [jnp.dot(k[:, a * qp:(a + 1) * qp], qt_ref[0, 0, a * qp:(a + 1) * qp, :],
                      preferred_element_type=F32) for a in range(2)]
        for a in range(2):
            s = ss[a]
            if kind == DIAG:
                keys = lax.broadcasted_iota(jnp.int32, (t, t), 0)
                queries = lax.broadcasted_iota(jnp.int32, (t, t), 1)
                s = jnp.where((keys // CHUNK) <= (queries // CHUNK), s, MASK_VALUE)
            _stage_logits(s, s_ref.at[slot, a], mb_ref.at[slot, a])

    def absorb(j, slot):
        vt = vt_ref[0, j]
        for a in range(2):
            _softmax_step_t(s_ref.at[slot, a], mb_ref.at[slot, a], vt[a * dv:(a + 1) * dv, :],
                            m_ref.at[a], l_ref.at[a], acc_ref.at[a])

    _sweep_key_tiles(i, logits, absorb)
    o = jnp.concatenate([acc_ref[0] / l_ref[0], acc_ref[1] / l_ref[1]], axis=0)
    o_ref[0] = o.T.astype(BF16)


def _rotate_half_cols(w):
    half = MLA_ROPE // 2
    return jnp.concatenate([-w[..., half:], w[..., :half]], axis=-1)


def _mla_mixer(x3, g, w_a, q_a_norm, kv_a_norm, w_uq, w_ukv, q_norm_g, k_norm_g, w_out):
    b, s, d = x3.shape
    t = TOK_TILE
    nb = s // t
    hh = MLA_HEADS
    nope, rope, qp, dv = MLA_NOPE, MLA_ROPE, MLA_QK_PAD, MLA_V
    half = rope // 2
    wa_q = w_a[:, :MLA_Q_LORA].astype(BF16)
    wa_kv = w_a[:, MLA_Q_LORA:MLA_Q_LORA + MLA_KV_LORA].astype(BF16)
    wa_r = w_a[:, MLA_Q_LORA + MLA_KV_LORA:]
    war = jnp.concatenate([jnp.zeros((d, nope), F32), wa_r, _rotate_half_cols(wa_r)], axis=1).astype(BF16)
    wq3 = w_uq.reshape(MLA_Q_LORA, hh, nope + rope)
    wuqt = jnp.concatenate([wq3, _rotate_half_cols(wq3[:, :, nope:])], axis=2
                           ).reshape(MLA_Q_LORA, hh * qp).T.astype(BF16)
    wkv3 = w_ukv.reshape(MLA_KV_LORA, hh, nope + dv)
    wuk = jnp.concatenate([wkv3[:, :, :nope], jnp.zeros((MLA_KV_LORA, hh, 2 * rope), F32)], axis=2
                          ).reshape(MLA_KV_LORA, hh * qp).astype(BF16)
    wuvt = wkv3[:, :, nope:].reshape(MLA_KV_LORA, hh * dv).T.astype(BF16)
    swap = lambda v: jnp.concatenate([v[half:], v[:half]])
    scale = (nope + rope) ** -0.5
    gq_head = jnp.concatenate([q_norm_g, swap(q_norm_g[nope:])]) * (scale * LOG2E)
    gq = jnp.tile(gq_head, hh).reshape(hh * qp, 1)
    gkn = jnp.tile(jnp.concatenate([k_norm_g[:nope], jnp.zeros((2 * rope,), F32)]), hh).reshape(1, hh * qp)
    gkr = jnp.concatenate([jnp.zeros((nope,), F32), k_norm_g[nope:], swap(k_norm_g[nope:])]).reshape(1, qp)
    inv = ROPE_BASE ** (-jnp.arange(0, rope, 2, dtype=F32) / rope)
    ang = jnp.arange(s, dtype=F32)[:, None] * inv[None, :]
    cos, sin = jnp.cos(ang), jnp.sin(ang)
    cs = jnp.concatenate([cos, cos, sin, sin], axis=1)
    csk = jnp.concatenate([jnp.zeros((s, nope), F32), cs], axis=1)
    gmat = _group_mean_matrix([nope, rope, rope], 2 * qp)
    feat_major = lambda w: pl.BlockSpec((1, 1, w, t), lambda bi, i: (bi, i, 0, 0))
    qt, k, vt = pl.pallas_call(
        _mla_proj_kernel,
        grid=(b, nb),
        in_specs=[pl.BlockSpec((1, t, d), lambda bi, i: (bi, i, 0)),
                  _const_spec((1, d)),
                  _const_spec((d, MLA_Q_LORA)),
                  _const_spec((d, MLA_KV_LORA)),
                  _const_spec((d, qp)),
                  _const_spec((1, MLA_Q_LORA)),
                  _const_spec((1, MLA_KV_LORA)),
                  _const_spec((hh * qp, MLA_Q_LORA)),
                  _const_spec((MLA_KV_LORA, hh * qp)),
                  _const_spec((hh * dv, MLA_KV_LORA)),
                  _const_spec((2 * qp, 2 * qp)),
                  _const_spec((hh * qp, 1)),
                  pl.BlockSpec((2 * rope, t), lambda bi, i: (0, i)),
                  _const_spec((1, hh * qp)),
                  _const_spec((1, qp)),
                  pl.BlockSpec((t, qp), lambda bi, i: (i, 0))],
        out_specs=[feat_major(hh * qp),
                   pl.BlockSpec((1, t, hh * qp), lambda bi, i: (bi, i, 0)),
                   feat_major(hh * dv)],
        out_shape=[jax.ShapeDtypeStruct((b, nb, hh * qp, t), BF16),
                   jax.ShapeDtypeStruct((b, s, hh * qp), BF16),
                   jax.ShapeDtypeStruct((b, nb, hh * dv, t), BF16)],
        compiler_params=_params(("parallel", "parallel")),
        name="mla_proj",
    )(x3, g.reshape(1, d), wa_q, wa_kv, war, q_a_norm.reshape(1, -1), kv_a_norm.reshape(1, -1),
      wuqt, wuk, wuvt, gmat, gq, cs.T, gkn, gkr, csk)

    o = pl.pallas_call(
        _mla_attn_kernel,
        grid=(b, hh // 2, nb),
        in_specs=[pl.BlockSpec((1, 1, 2 * qp, t), lambda bi, h, i: (bi, i, h, 0)),
                  pl.BlockSpec((1, s, 2 * qp), lambda bi, h, i: (bi, 0, h)),
                  pl.BlockSpec((1, nb, 2 * dv, t), lambda bi, h, i: (bi, 0, h, 0))],
        out_specs=pl.BlockSpec((1, t, 2 * dv), lambda bi, h, i: (bi, i, h)),
        out_shape=jax.ShapeDtypeStruct((b, s, hh * dv), BF16),
        scratch_shapes=[pltpu.VMEM((2, 1, t), F32), pltpu.VMEM((2, 1, t), F32),
                        pltpu.VMEM((2, dv, t), F32),
                        pltpu.VMEM((2, 2, t, t), F32),
                        pltpu.VMEM((2, 2, 1, t), F32)],
        compiler_params=_params(("parallel", "parallel", "arbitrary")),
        name="mla_attn",
    )(qt, k, vt)
    return o.reshape(b * s, hh * dv), w_out


def kernel(x, rel_bias_table, ffn_norm, ffn_w_in, ffn_w_out, mixer_norm, conv_w_in, conv_b_in, conv_w_dw, conv_b_dw, conv_ln_g, conv_ln_b, conv_w_out, conv_b_out, diff_w_in, diff_q_norm, diff_k_norm, diff_lambda, diff_sub_norm, diff_w_out, sb_w_in, sb_w_out, mla_w_a, mla_q_a_norm, mla_kv_a_norm, mla_w_uq, mla_w_ukv, mla_q_norm, mla_k_norm, mla_w_out):
    b, s, d = x.shape
    assert d == D_MODEL and s % TOK_TILE == 0
    depth = ffn_norm.shape[0]
    ffn = lambda xx, i, k, proj=None: _ffn(xx.reshape(b * s, d), ffn_norm[i, k], ffn_w_in, ffn_w_out,
                                           i, k, proj).reshape(b, s, d)
    for i in range(depth):
        mixer, j = i % N_MIXERS, i // N_MIXERS
        x = ffn(x, i, 0)
        g = mixer_norm[i]
        proj = None
        if mixer == 0:
            x = _conv_mixer(x, g, conv_w_in[j], conv_b_in[j], conv_w_dw[j], conv_b_dw[j],
                            conv_ln_g[j], conv_ln_b[j], conv_w_out[j], conv_b_out[j])
        elif mixer == 1:
            proj = _diff_mixer(x, i, g, diff_w_in[j], diff_q_norm[j], diff_k_norm[j], diff_lambda[j],
                               diff_sub_norm[j], diff_w_out[j], rel_bias_table)
        elif mixer == 2:
            proj = _sb_mixer(x, g, sb_w_in[j], sb_w_out[j])
        else:
            proj = _mla_mixer(x, g, mla_w_a[j], mla_q_a_norm[j], mla_kv_a_norm[j], mla_w_uq[j],
                              mla_w_ukv[j], mla_q_norm[j], mla_k_norm[j], mla_w_out[j])
        x = ffn(x, i, 1, proj)
    return x
```

```python
import functools
import math

import jax
import jax.numpy as jnp
from jax import lax
from jax.experimental import pallas as pl
from jax.experimental.pallas import tpu as pltpu

F32 = jnp.float32
BF16 = jnp.bfloat16

D_MODEL = 1024
DEPTH = 4
CHUNK = 64
N_MIXERS = 4
D_FF = 2816
RMS_EPS = 1e-6
LN_EPS = 1e-5
MASK_VALUE = -1e30
CONV_WIDTH = 31
DIFF_HEADS = 8
DIFF_QK_DIM = 64
DIFF_V_DIM = 128
LAMBDA_INIT_BASE = 0.8
LAMBDA_INIT_SCALE = 0.6
LAMBDA_INIT_DECAY = 0.3
REL_BUCKETS = 32
REL_MAX_DIST = 128
SB_HEADS = 16
SB_HEAD_DIM = 64
MLA_HEADS = 16
MLA_Q_LORA = 384
MLA_KV_LORA = 256
MLA_NOPE = 64
MLA_ROPE = 32
MLA_V = 64
ROPE_BASE = 10000.0
LOG2E = math.log2(math.e)

LANES = 128
SUBLANES = 8
TOK_TILE = 512
FF_CHUNK = 256
FAR_UNROLL = 4
SB_KEY_BLOCK = 128
SB_SWEEP_KEYS = 256
CONV_HALO = 32
CONV_ROWS = 16
VMEM_LIMIT = 56 * 1024 * 1024
F32_EXP2_ZERO = -151.0


def _params(sem):
    return pltpu.CompilerParams(dimension_semantics=sem, vmem_limit_bytes=VMEM_LIMIT)


def _const_spec(shape):
    nd = len(shape)
    return pl.BlockSpec(shape, lambda *_: (0,) * nd, pipeline_mode=pl.Buffered(1))


def _rms_bf16(x, g):
    ms = jnp.mean(x * x, axis=-1, keepdims=True)
    return (x * lax.rsqrt(ms + RMS_EPS) * g).astype(BF16)


def _split_dot(x, w):
    hi = x.astype(BF16)
    lo = (x - hi.astype(F32)).astype(BF16)
    return (jnp.dot(hi, w, preferred_element_type=F32)
            + jnp.dot(lo, w, preferred_element_type=F32))


def _dot_nt(a, b):
    return lax.dot_general(a, b, (((1,), (1,)), ((), ())), preferred_element_type=F32)


def _rows_rms(y, g_col):
    ms = jnp.mean(y * y, axis=0, keepdims=True)
    return y * lax.rsqrt(ms + RMS_EPS) * g_col


def _ffn_kernel(*refs, fused_proj):
    if fused_proj:
        x_ref, a_ref, wa_ref, g_ref, win_ref, wout_ref, o_ref, gate_ref = refs
        x = x_ref[...] + jnp.dot(a_ref[...], wa_ref[...], preferred_element_type=F32)
    else:
        x_ref, g_ref, win_ref, wout_ref, o_ref, gate_ref = refs
        x = x_ref[...]
    h = _rms_bf16(x, g_ref[...])
    for c in range(D_FF // FF_CHUNK):
        lo, hi = c * FF_CHUNK, (c + 1) * FF_CHUNK
        a = jnp.dot(h, win_ref[:, lo:hi].astype(BF16), preferred_element_type=F32)
        u = jnp.dot(h, win_ref[:, D_FF + lo:D_FF + hi].astype(BF16), preferred_element_type=F32)
        gate_ref[:, lo:hi] = (a * jax.nn.sigmoid(a) * u).astype(BF16)
    y = jnp.dot(gate_ref[...], wout_ref[...].astype(BF16), preferred_element_type=F32)
    o_ref[...] = x + 0.5 * y


def _ffn(x2, g, w_in_all, w_out_all, layer, half, proj=None):
    n, d = x2.shape
    tm = TOK_TILE
    rows = lambda w: pl.BlockSpec((tm, w), lambda i: (i, 0))
    pick = lambda r, c: pl.BlockSpec((None, None, r, c), lambda i: (layer, half, 0, 0),
                                     pipeline_mode=pl.Buffered(1))
    in_specs, args = [rows(d)], [x2]
    if proj is not None:
        a, w_a = proj
        in_specs += [rows(a.shape[1]), _const_spec(w_a.shape)]
        args += [a, w_a.astype(BF16)]
    in_specs += [_const_spec((1, d)), pick(d, 2 * D_FF), pick(D_FF, d)]
    args += [g.reshape(1, d), w_in_all, w_out_all]
    return pl.pallas_call(
        functools.partial(_ffn_kernel, fused_proj=proj is not None),
        grid=(n // tm,),
        in_specs=in_specs,
        out_specs=rows(d),
        out_shape=jax.ShapeDtypeStruct((n, d), F32),
        scratch_shapes=[pltpu.VMEM((tm, D_FF), BF16)],
        compiler_params=_params(("parallel",)),
        name="ffn_proj" if proj is not None else "ffn",
    )(*args)


def _conv_in_kernel(x_ref, g_ref, w_ref, b_ref, u_ref):
    d = D_MODEL
    h = _rms_bf16(x_ref[...], g_ref[...])
    y = jnp.dot(h, w_ref[...], preferred_element_type=F32) + b_ref[...]
    u_ref[...] = y[:, :d] * jax.nn.sigmoid(y[:, d:])


def _conv_out_kernel(x_ref, ucur_ref, uprev_ref, wdw_ref, bdw_ref, lng_ref, lnb_ref,
                     wout_ref, bout_ref, y_ref, ext_ref, act_ref):
    ts = ucur_ref.shape[1]
    i = pl.program_id(1)
    ext_ref[0, 0:CONV_HALO, :] = jnp.where(i > 0, uprev_ref[0], 0.0)
    ext_ref[0, CONV_HALO:, :] = ucur_ref[0]
    n_shift = ts + CONV_HALO - SUBLANES
    for p in range(1, SUBLANES):
        ext_ref[p, 0:n_shift, :] = ext_ref[0, p:p + n_shift, :]
    off = CONV_HALO - (CONV_WIDTH - 1)
    for r in range(ts // CONV_ROWS):
        r0 = r * CONV_ROWS
        acc = None
        for k in range(CONV_WIDTH):
            p = (off + k) % SUBLANES
            a = r0 + off + k - p
            win = ext_ref[p, a:a + CONV_ROWS, :].reshape(CONV_ROWS // SUBLANES, SUBLANES, -1)
            tap = win * wdw_ref[k]
            acc = tap if acc is None else acc + tap
        c = acc.reshape(CONV_ROWS, -1) + bdw_ref[...]
        mu = jnp.mean(c, axis=-1, keepdims=True)
        cc = c - mu
        var = jnp.mean(cc * cc, axis=-1, keepdims=True)
        ln = cc * lax.rsqrt(var + LN_EPS) * lng_ref[...] + lnb_ref[...]
        act_ref[r0:r0 + CONV_ROWS, :] = (ln * jax.nn.sigmoid(ln)).astype(BF16)
    y = jnp.dot(act_ref[...], wout_ref[...], preferred_element_type=F32) + bout_ref[...]
    y_ref[0] = x_ref[0] + y


def _conv_mixer(x3, g, w_in, b_in, w_dw, b_dw, ln_g, ln_b, w_out, b_out):
    b, s, d = x3.shape
    n = b * s
    tm = TOK_TILE
    u = pl.pallas_call(
        _conv_in_kernel,
        grid=(n // tm,),
        in_specs=[pl.BlockSpec((tm, d), lambda i: (i, 0)),
                  _const_spec((1, d)),
                  _const_spec((d, 2 * d)),
                  _const_spec((1, 2 * d))],
        out_specs=pl.BlockSpec((tm, d), lambda i: (i, 0)),
        out_shape=jax.ShapeDtypeStruct((n, d), F32),
        compiler_params=_params(("parallel",)),
        name="conv_in",
    )(x3.reshape(n, d), g.reshape(1, d), w_in.astype(BF16), b_in.reshape(1, 2 * d))
    u3 = u.reshape(b, s, d)
    halo_per_tile = tm // CONV_HALO
    row = lambda v: v.reshape(1, d)
    return pl.pallas_call(
        _conv_out_kernel,
        grid=(b, s // tm),
        in_specs=[pl.BlockSpec((1, tm, d), lambda bi, i: (bi, i, 0)),
                  pl.BlockSpec((1, tm, d), lambda bi, i: (bi, i, 0)),
                  pl.BlockSpec((1, CONV_HALO, d),
                               lambda bi, i: (bi, jnp.maximum(i * halo_per_tile - 1, 0), 0)),
                  _const_spec((CONV_WIDTH, SUBLANES, d)),
                  _const_spec((1, d)), _const_spec((1, d)), _const_spec((1, d)),
                  _const_spec((d, d)),
                  _const_spec((1, d))],
        out_specs=pl.BlockSpec((1, tm, d), lambda bi, i: (bi, i, 0)),
        out_shape=jax.ShapeDtypeStruct((b, s, d), F32),
        scratch_shapes=[pltpu.VMEM((SUBLANES, tm + CONV_HALO, d), F32), pltpu.VMEM((tm, d), BF16)],
        compiler_params=_params(("parallel", "parallel")),
        name="conv_out",
    )(x3, u3, u3, jnp.broadcast_to(w_dw[:, None, :], (CONV_WIDTH, SUBLANES, d)), row(b_dw),
      row(ln_g), row(ln_b), w_out.astype(BF16), row(b_out))


def _stage_logits(s, s_ref, mb_ref):
    s_ref[...] = s
    mb_ref[...] = jnp.max(s, axis=0, keepdims=True)


def _softmax_step_t(s_ref, mb_ref, vt, m_ref, l_ref, acc_ref):
    m_prev = m_ref[...]
    m_new = jnp.maximum(m_prev, mb_ref[...])
    alpha = jnp.exp2(m_prev - m_new)
    p = jnp.exp2(s_ref[...] - m_new)
    l_ref[...] = alpha * l_ref[...] + jnp.sum(p, axis=0, keepdims=True)
    acc_ref[...] = alpha * acc_ref[...] + jnp.dot(vt, p.astype(BF16), preferred_element_type=F32)
    m_ref[...] = m_new


FAR, PREV, DIAG = "far", "prev", "diag"


def _sweep_key_tiles(i, logits, absorb):
    assert FAR_UNROLL == 4, "the remainder below is handled as one pair plus one single tile"
    n_far = jnp.maximum(i - 1, 0)
    n_groups = n_far // FAR_UNROLL

    def far_group(jg, carry):
        base = jg * FAR_UNROLL
        logits(base, FAR, 0)
        for u in range(FAR_UNROLL):
            if u + 1 < FAR_UNROLL:
                logits(base + u + 1, FAR, (u + 1) % 2)
            absorb(base + u, u % 2)
        return carry

    lax.fori_loop(0, n_groups, far_group, 0)
    rest = n_groups * FAR_UNROLL

    @pl.when(n_far - rest >= 2)
    def _far_pair():
        logits(rest, FAR, 0)
        logits(rest + 1, FAR, 1)
        absorb(rest, 0)
        absorb(rest + 1, 1)

    @pl.when((n_far - rest) % 2 == 1)
    def _far_single():
        logits(n_far - 1, FAR, 0)
        absorb(n_far - 1, 0)

    @pl.when(i >= 1)
    def _last_two():
        logits(i - 1, PREV, 0)
        logits(i, DIAG, 1)
        absorb(i - 1, 0)
        absorb(i, 1)

    @pl.when(i == 0)
    def _only_diag():
        logits(i, DIAG, 0)
        absorb(i, 0)


def _group_mean_matrix(groups, width):
    idx = []
    for gi, size in enumerate(groups):
        idx += [gi] * size
    reps = width // len(idx)
    gid = jnp.asarray([r * len(groups) + g for r in range(reps) for g in idx], jnp.int32)
    sizes = jnp.asarray([float(groups[g]) for _ in range(reps) for g in idx], F32)
    same = gid[:, None] == gid[None, :]
    return jnp.where(same, 1.0 / sizes[None, :], 0.0).astype(BF16)


def _diff_proj_kernel(x_ref, g_ref, wqt_ref, wk_ref, wvt_ref, gq_ref, gk_ref, gmat_ref,
                      qt_ref, k_ref, vt_ref):
    h = _rms_bf16(x_ref[0], g_ref[...])
    yqt = _dot_nt(wqt_ref[...], h)
    for r in range(yqt.shape[0] // DIFF_QK_DIM):
        sl = slice(r * DIFF_QK_DIM, (r + 1) * DIFF_QK_DIM)
        qt_ref[0, 0, sl, :] = _rows_rms(yqt[sl, :], gq_ref[sl, :]).astype(BF16)
    yk = jnp.dot(h, wk_ref[...], preferred_element_type=F32)
    width = gmat_ref.shape[0]
    for c in range(yk.shape[1] // width):
        sl = slice(c * width, (c + 1) * width)
        y = yk[:, sl]
        ms = _split_dot(y * y, gmat_ref[...])
        k_ref[0, :, sl] = (y * lax.rsqrt(ms + RMS_EPS) * gk_ref[:, sl]).astype(BF16)
    vt_ref[0, 0] = _dot_nt(wvt_ref[...], h).astype(BF16)


def _diff_attn_kernel(tab_ref, scal_ref, qt_ref, k_ref, vt_ref, bucket_ref, subg_ref, o_ref,
                      bias_ref, m_ref, l_ref, acc_ref, s_ref, mb_ref):
    t = qt_ref.shape[3]
    hd = pl.program_id(0)
    i = pl.program_id(2)

    @pl.when(jnp.logical_and(pl.program_id(1) == 0, i == 0))
    def _build_bias():
        far = tab_ref[REL_BUCKETS // 2 - 1, hd]
        keys = lax.broadcasted_iota(jnp.int32, (t, t), 0)
        queries = lax.broadcasted_iota(jnp.int32, (t, t), 1)
        visible = (keys // CHUNK) <= (queries // CHUNK)
        for tile in range(2):
            bk = bucket_ref[tile]
            bias = jnp.zeros((t, t), F32)
            for b in range(REL_BUCKETS):
                bias = jnp.where(bk == b, (tab_ref[b, hd] - far) * LOG2E, bias)
            if tile == 1:
                bias = jnp.where(visible, bias, MASK_VALUE)
            bias_ref[tile] = bias

    m_ref[...] = jnp.full(m_ref.shape, -jnp.inf, F32)
    l_ref[...] = jnp.zeros(l_ref.shape, F32)
    acc_ref[...] = jnp.zeros(acc_ref.shape, F32)

    qt = qt_ref[0, 0]
    feat = lax.broadcasted_iota(jnp.int32, qt.shape, 0)
    qmaps = [jnp.where((feat // DIFF_QK_DIM) == mp, qt, jnp.zeros_like(qt)) for mp in range(2)]

    bias_tile = {FAR: None, PREV: 0, DIAG: 1}

    def logits(j, kind, slot):
        k = k_ref[0, pl.ds(pl.multiple_of(j * t, t), t), :]
        ss = [jnp.dot(k, qmaps[mp], preferred_element_type=F32) for mp in range(2)]
        for mp in range(2):
            s = ss[mp] if kind == FAR else ss[mp] + bias_ref[bias_tile[kind]]
            _stage_logits(s, s_ref.at[slot, mp], mb_ref.at[slot, mp])

    def absorb(j, slot):
        vt = vt_ref[0, j]
        for mp in range(2):
            _softmax_step_t(s_ref.at[slot, mp], mb_ref.at[slot, mp], vt,
                            m_ref.at[mp], l_ref.at[mp], acc_ref.at[mp])

    _sweep_key_tiles(i, logits, absorb)

    lam = scal_ref[0]
    attn = acc_ref[0] / l_ref[0] - lam * (acc_ref[1] / l_ref[1])
    o_ref[0] = _rows_rms(attn, subg_ref[...]).T.astype(BF16)


def _t5_bucket(rel):
    nb = REL_BUCKETS // 2
    bucket = jnp.where(rel > 0, nb, 0)
    n = jnp.abs(rel)
    max_exact = nb // 2
    n_f = jnp.maximum(n, 1).astype(jnp.float32)
    large = max_exact + (jnp.log(n_f / max_exact) / math.log(REL_MAX_DIST / max_exact)
                         * (nb - max_exact)).astype(jnp.int32)
    large = jnp.minimum(large, nb - 1)
    return bucket + jnp.where(n < max_exact, n, large)


def _diff_mixer(x3, layer_idx, g, w_in, q_norm_g, k_norm_g, lam, sub_norm_g, w_out, rel_table):
    b, s, d = x3.shape
    t = TOK_TILE
    nb = s // t
    hh, dq, dv = DIFF_HEADS, DIFF_QK_DIM, DIFF_V_DIM
    w3 = w_in.reshape(d, hh, 4 * dq + dv)
    wqt = w3[:, :, :2 * dq].reshape(d, hh * 2 * dq).T.astype(BF16)
    wk = w3[:, :, 2 * dq:4 * dq].reshape(d, hh * 2 * dq).astype(BF16)
    wvt = w3[:, :, 4 * dq:].reshape(d, hh * dv).T.astype(BF16)
    scale = dq ** -0.5
    gq = (jnp.tile(q_norm_g, 2 * hh) * (scale * LOG2E)).reshape(hh * 2 * dq, 1)
    gk = jnp.tile(k_norm_g, 2 * hh).reshape(1, hh * 2 * dq)
    gmat = _group_mean_matrix([dq], 2 * LANES)
    feat_major = lambda w: pl.BlockSpec((1, 1, w, t), lambda bi, i: (bi, i, 0, 0))
    qt, k, vt = pl.pallas_call(
        _diff_proj_kernel,
        grid=(b, nb),
        in_specs=[pl.BlockSpec((1, t, d), lambda bi, i: (bi, i, 0)),
                  _const_spec((1, d)),
                  _const_spec((hh * 2 * dq, d)),
                  _const_spec((d, hh * 2 * dq)),
                  _const_spec((hh * dv, d)),
                  _const_spec((hh * 2 * dq, 1)),
                  _const_spec((1, hh * 2 * dq)),
                  _const_spec((2 * LANES, 2 * LANES))],
        out_specs=[feat_major(hh * 2 * dq),
                   pl.BlockSpec((1, t, hh * 2 * dq), lambda bi, i: (bi, i, 0)),
                   feat_major(hh * dv)],
        out_shape=[jax.ShapeDtypeStruct((b, nb, hh * 2 * dq, t), BF16),
                   jax.ShapeDtypeStruct((b, s, hh * 2 * dq), BF16),
                   jax.ShapeDtypeStruct((b, nb, hh * dv, t), BF16)],
        compiler_params=_params(("parallel", "parallel")),
        name="diff_proj",
    )(x3, g.reshape(1, d), wqt, wk, wvt, gq, gk, gmat)

    lam_init = LAMBDA_INIT_BASE - LAMBDA_INIT_SCALE * math.exp(-LAMBDA_INIT_DECAY * layer_idx)
    lam_full = (jnp.exp(jnp.sum(lam[0] * lam[1]).astype(F32))
                - jnp.exp(jnp.sum(lam[2] * lam[3]).astype(F32)) + lam_init)
    scal = jnp.reshape(lam_full, (1,)).astype(F32)
    r = jnp.arange(t)
    rel_diag = r[:, None] - r[None, :]
    buckets = jnp.stack([_t5_bucket(rel_diag - t), _t5_bucket(rel_diag)]).astype(jnp.int32)
    subg = (sub_norm_g * (1.0 - lam_init)).reshape(dv, 1)
    smem = pl.BlockSpec(memory_space=pltpu.SMEM)
    o = pl.pallas_call(
        _diff_attn_kernel,
        grid=(hh, b, nb),
        in_specs=[smem, smem,
                  pl.BlockSpec((1, 1, 2 * dq, t), lambda h, bi, i: (bi, i, h, 0)),
                  pl.BlockSpec((1, s, 2 * dq), lambda h, bi, i: (bi, 0, h)),
                  pl.BlockSpec((1, nb, dv, t), lambda h, bi, i: (bi, 0, h, 0)),
                  _const_spec((2, t, t)),
                  _const_spec((dv, 1))],
        out_specs=pl.BlockSpec((1, t, dv), lambda h, bi, i: (bi, i, h)),
        out_shape=jax.ShapeDtypeStruct((b, s, hh * dv), BF16),
        scratch_shapes=[pltpu.VMEM((2, t, t), F32),
                        pltpu.VMEM((2, 1, t), F32),
                        pltpu.VMEM((2, 1, t), F32),
                        pltpu.VMEM((2, dv, t), F32),
                        pltpu.VMEM((2, 2, t, t), F32),
                        pltpu.VMEM((2, 2, 1, t), F32)],
        compiler_params=_params(("parallel", "arbitrary", "arbitrary")),
        name="diff_attn",
    )(rel_table.astype(F32), scal, qt, k, vt, buckets, subg)
    return o.reshape(b * s, hh * dv), w_out


def _sb_proj_kernel(x_ref, g_ref, wqt_ref, wk_ref, wvt_ref, qt_ref, k_ref, vt_ref):
    h = _rms_bf16(x_ref[0], g_ref[...])
    qt_ref[0, 0] = (_dot_nt(wqt_ref[...], h) * (SB_HEAD_DIM ** -0.5 * LOG2E)).astype(BF16)
    k_ref[0] = jnp.dot(h, wk_ref[...], preferred_element_type=F32).astype(BF16)
    yvt = _dot_nt(wvt_ref[...], h).astype(BF16)
    sk = vt_ref.shape[3]
    for c in range(vt_ref.shape[1]):
        vt_ref[0, c] = yvt[:, c * sk:(c + 1) * sk]


def _sb_attn_kernel(qt_ref, k_ref, vt_ref, tri_ref, o_ref, run_ref, acc_ref, base_ref, tot_ref):
    t = qt_ref.shape[3]
    kb = SB_KEY_BLOCK
    sk = SB_SWEEP_KEYS
    dh = SB_HEAD_DIM
    i = pl.program_id(2)
    sub = sk // kb
    qt = qt_ref[0, 0]
    feat = lax.broadcasted_iota(jnp.int32, qt.shape, 0)
    qh = [jnp.where((feat // dh) == a, qt, jnp.zeros_like(qt)) for a in range(2)]
    run_ref[...] = jnp.zeros(run_ref.shape, F32)
    acc_ref[...] = jnp.zeros(acc_ref.shape, F32)

    def logits(j, c0):
        k = k_ref[0, pl.ds(pl.multiple_of(j * sk, sk), sk), :]
        return [jnp.dot(k, qh[a][:, c0:], preferred_element_type=F32) for a in range(2)]

    def causal_mask(c0):
        shape = (sk, t - c0)
        return lax.broadcasted_iota(jnp.int32, shape, 0) < lax.broadcasted_iota(jnp.int32, shape, 1)

    def prepare(zs, diagonal, c0, slot):
        qs = slice(c0, t)
        for a in range(2):
            z = zs[a]
            soft = jnp.log2(1.0 + jnp.exp2(-jnp.abs(z)))
            log_beta = jnp.minimum(z, 0.0) - soft
            log_keep = log_beta - z
            if diagonal:
                log_keep = jnp.where(causal_mask(c0), log_keep, 0.0)
            hi = log_keep.astype(BF16)
            lo = (log_keep - hi.astype(F32)).astype(BF16)
            for blk in range(sub):
                rows = slice(blk * kb, (blk + 1) * kb)
                sums = jnp.dot(tri_ref[...], jnp.concatenate([hi[rows], lo[rows]], axis=0),
                               preferred_element_type=F32)
                base_ref[slot, a, rows, qs] = log_beta[rows] + sums[:kb]
                tot_ref[slot, a, blk, :, qs] = sums[kb:]

    def fold(j, diagonal, c0, slot):
        qs = slice(c0, t)
        vt = vt_ref[0, j]
        for a in range(2):
            run = run_ref[a, :, qs]
            ws = [None] * sub
            for blk in range(sub - 1, -1, -1):
                rows = slice(blk * kb, (blk + 1) * kb)
                ws[blk] = jnp.exp2(base_ref[slot, a, rows, qs] + run[0:1])
                run = run + tot_ref[slot, a, blk, :, qs]
            w = jnp.concatenate(ws, axis=0)
            if diagonal:
                w = jnp.where(causal_mask(c0), w, 0.0)
            acc_ref[a, :, qs] += jnp.dot(vt[a * dh:(a + 1) * dh, :], w.astype(BF16),
                                         preferred_element_type=F32)
            run_ref[a, :, qs] = run

    per_tile = t // sk
    own = [(i * per_tile + b, True, b * sk) for b in range(per_tile - 1, -1, -1)]

    def run_blocks(blocks):
        zs = logits(blocks[0][0], blocks[0][2])
        for n, (j, diagonal, c0) in enumerate(blocks):
            nxt = logits(blocks[n + 1][0], blocks[n + 1][2]) if n + 1 < len(blocks) else None
            prepare(zs, diagonal, c0, n)
            if n >= 1:
                fold(blocks[n - 1][0], blocks[n - 1][1], blocks[n - 1][2], n - 1)
            zs = nxt
        last = len(blocks) - 1
        fold(blocks[last][0], blocks[last][1], blocks[last][2], last)

    @pl.when(i >= 1)
    def _with_previous():
        run_blocks(own + [(i * per_tile - 1, False, 0)])

    @pl.when(i == 0)
    def _first_tile():
        run_blocks(own)

    def cond(c):
        j, top = c
        return jnp.logical_and(j >= 0, top > F32_EXP2_ZERO)

    def body(c):
        j, _ = c
        prepare(logits(j, 0), False, 0, 0)
        fold(j, False, 0, 0)
        return j - 1, jnp.max(run_ref[...])

    lax.while_loop(cond, body, (i * per_tile - 2, jnp.max(run_ref[...])))
    o_ref[0] = jnp.concatenate([acc_ref[0], acc_ref[1]], axis=0).T.astype(BF16)


def _sb_mixer(x3, g, w_in, w_out):
    b, s, d = x3.shape
    t = TOK_TILE
    kb = SB_KEY_BLOCK
    sweep = SB_SWEEP_KEYS
    nb = s // t
    hh, dh = SB_HEADS, SB_HEAD_DIM
    w3 = w_in.reshape(d, hh, 3 * dh)
    wqt = w3[:, :, :dh].reshape(d, hh * dh).T.astype(BF16)
    wk = w3[:, :, dh:2 * dh].reshape(d, hh * dh).astype(BF16)
    wvt = w3[:, :, 2 * dh:].reshape(d, hh * dh).T.astype(BF16)
    qt, k, vt = pl.pallas_call(
        _sb_proj_kernel,
        grid=(b, nb),
        in_specs=[pl.BlockSpec((1, t, d), lambda bi, i: (bi, i, 0)),
                  _const_spec((1, d)),
                  _const_spec((hh * dh, d)),
                  _const_spec((d, hh * dh)),
                  _const_spec((hh * dh, d))],
        out_specs=[pl.BlockSpec((1, 1, hh * dh, t), lambda bi, i: (bi, i, 0, 0)),
                   pl.BlockSpec((1, t, hh * dh), lambda bi, i: (bi, i, 0)),
                   pl.BlockSpec((1, t // sweep, hh * dh, sweep), lambda bi, i: (bi, i, 0, 0))],
        out_shape=[jax.ShapeDtypeStruct((b, nb, hh * dh, t), BF16),
                   jax.ShapeDtypeStruct((b, s, hh * dh), BF16),
                   jax.ShapeDtypeStruct((b, s // sweep, hh * dh, sweep), BF16)],
        compiler_params=_params(("parallel", "parallel")),
        name="sb_proj",
    )(x3, g.reshape(1, d), wqt, wk, wvt)

    rj = jnp.arange(kb + SUBLANES)
    sk = jnp.arange(2 * kb) % kb
    tri = jnp.where((rj[:, None] >= kb) | (sk[None, :] > rj[:, None]), 1.0, 0.0).astype(BF16)
    pair = 2 * dh
    o = pl.pallas_call(
        _sb_attn_kernel,
        grid=(b, hh // 2, nb),
        in_specs=[pl.BlockSpec((1, 1, pair, t), lambda bi, h, i: (bi, i, h, 0)),
                  pl.BlockSpec((1, s, pair), lambda bi, h, i: (bi, 0, h)),
                  pl.BlockSpec((1, s // sweep, pair, sweep), lambda bi, h, i: (bi, 0, h, 0)),
                  _const_spec((kb + SUBLANES, 2 * kb))],
        out_specs=pl.BlockSpec((1, t, pair), lambda bi, h, i: (bi, i, h)),
        out_shape=jax.ShapeDtypeStruct((b, s, hh * dh), BF16),
        scratch_shapes=[pltpu.VMEM((2, SUBLANES, t), F32), pltpu.VMEM((2, dh, t), F32),
                        pltpu.VMEM((t // sweep + 1, 2, sweep, t), F32),
                        pltpu.VMEM((t // sweep + 1, 2, sweep // kb, SUBLANES, t), F32)],
        compiler_params=_params(("parallel", "parallel", "arbitrary")),
        name="sb_attn",
    )(qt, k, vt, tri)
    return o.reshape(b * s, hh * dh), w_out


MLA_QK_PAD = MLA_NOPE + 2 * MLA_ROPE


def _mla_proj_kernel(x_ref, g_ref, waq_ref, wakv_ref, war_ref, gqa_ref, gkva_ref, wuqt_ref,
                     wuk_ref, wuvt_ref, gmat_ref, gq_ref, cst_ref, gkn_ref, gkr_ref, csk_ref,
                     qt_ref, k_ref, vt_ref):
    nope, rope, qp = MLA_NOPE, MLA_ROPE, MLA_QK_PAD
    h = _rms_bf16(x_ref[0], g_ref[...])
    cq = _rms_bf16(jnp.dot(h, waq_ref[...], preferred_element_type=F32), gqa_ref[...])
    ckv = _rms_bf16(jnp.dot(h, wakv_ref[...], preferred_element_type=F32), gkva_ref[...])
    yqt = _dot_nt(wuqt_ref[...], cq)
    cs = cst_ref[...]
    for hd in range(MLA_HEADS):
        base = hd * qp
        y = yqt[base:base + qp, :]
        gcol = gq_ref[base:base + qp, :]
        qt_ref[0, 0, base:base + nope, :] = _rows_rms(y[:nope], gcol[:nope]).astype(BF16)
        msr = jnp.mean(y[nope:nope + rope] * y[nope:nope + rope], axis=0, keepdims=True)
        rr = y[nope:] * lax.rsqrt(msr + RMS_EPS) * gcol[nope:] * cs
        qf = (rr[:rope] + rr[rope:]).astype(BF16)
        qt_ref[0, 0, base + nope:base + nope + rope, :] = qf
        qt_ref[0, 0, base + nope + rope:base + qp, :] = qf
    width = gmat_ref.shape[0]
    gm = gmat_ref[...]
    sh = jnp.dot(h, war_ref[...], preferred_element_type=F32)
    ms = _split_dot(sh * sh, gm[:qp, :qp])
    shn = sh * lax.rsqrt(ms + RMS_EPS) * gkr_ref[...] * csk_ref[...]
    shared = jnp.concatenate([shn] * (width // qp), axis=1)
    ykn = jnp.dot(ckv, wuk_ref[...], preferred_element_type=F32)
    for c in range(ykn.shape[1] // width):
        sl = slice(c * width, (c + 1) * width)
        y = ykn[:, sl]
        ms = _split_dot(y * y, gm)
        k_ref[0, :, sl] = (y * lax.rsqrt(ms + RMS_EPS) * gkn_ref[:, sl] + shared).astype(BF16)
    vt_ref[0, 0] = _dot_nt(wuvt_ref[...], ckv).astype(BF16)


def _mla_attn_kernel(qt_ref, k_ref, vt_ref, o_ref, m_ref, l_ref, acc_ref, s_ref, mb_ref):
    t = qt_ref.shape[3]
    qp, dv = MLA_QK_PAD, MLA_V
    i = pl.program_id(2)
    m_ref[...] = jnp.full(m_ref.shape, -jnp.inf, F32)
    l_ref[...] = jnp.zeros(l_ref.shape, F32)
    acc_ref[...] = jnp.zeros(acc_ref.shape, F32)

    def logits(j, kind, slot):
        k = k_ref[0, pl.ds(pl.multiple_of(j * t, t), t), :]
        ss = [jnp.dot(k[:, a * qp:(a + 1) * qp], qt_ref[0, 0, a * qp:(a + 1) * qp, :],
                      preferred_element_type=F32) for a in range(2)]
        for a in range(2):
            s = ss[a]
            if kind == DIAG:
                keys = lax.broadcasted_iota(jnp.int32, (t, t), 0)
                queries = lax.broadcasted_iota(jnp.int32, (t, t), 1)
                s = jnp.where((keys // CHUNK) <= (queries // CHUNK), s, MASK_VALUE)
            _stage_logits(s, s_ref.at[slot, a], mb_ref.at[slot, a])

    def absorb(j, slot):
        vt = vt_ref[0, j]
        for a in range(2):
            _softmax_step_t(s_ref.at[slot, a], mb_ref.at[slot, a], vt[a * dv:(a + 1) * dv, :],
                            m_ref.at[a], l_ref.at[a], acc_ref.at[a])

    _sweep_key_tiles(i, logits, absorb)
    o = jnp.concatenate([acc_ref[0] / l_ref[0], acc_ref[1] / l_ref[1]], axis=0)
    o_ref[0] = o.T.astype(BF16)


def _rotate_half_cols(w):
    half = MLA_ROPE // 2
    return jnp.concatenate([-w[..., half:], w[..., :half]], axis=-1)


def _mla_mixer(x3, g, w_a, q_a_norm, kv_a_norm, w_uq, w_ukv, q_norm_g, k_norm_g, w_out):
    b, s, d = x3.shape
    t = TOK_TILE
    nb = s // t
    hh = MLA_HEADS
    nope, rope, qp, dv = MLA_NOPE, MLA_ROPE, MLA_QK_PAD, MLA_V
    half = rope // 2
    wa_q = w_a[:, :MLA_Q_LORA].astype(BF16)
    wa_kv = w_a[:, MLA_Q_LORA:MLA_Q_LORA + MLA_KV_LORA].astype(BF16)
    wa_r = w_a[:, MLA_Q_LORA + MLA_KV_LORA:]
    war = jnp.concatenate([jnp.zeros((d, nope), F32), wa_r, _rotate_half_cols(wa_r)], axis=1).astype(BF16)
    wq3 = w_uq.reshape(MLA_Q_LORA, hh, nope + rope)
    wuqt = jnp.concatenate([wq3, _rotate_half_cols(wq3[:, :, nope:])], axis=2
                           ).reshape(MLA_Q_LORA, hh * qp).T.astype(BF16)
    wkv3 = w_ukv.reshape(MLA_KV_LORA, hh, nope + dv)
    wuk = jnp.concatenate([wkv3[:, :, :nope], jnp.zeros((MLA_KV_LORA, hh, 2 * rope), F32)], axis=2
                          ).reshape(MLA_KV_LORA, hh * qp).astype(BF16)
    wuvt = wkv3[:, :, nope:].reshape(MLA_KV_LORA, hh * dv).T.astype(BF16)
    swap = lambda v: jnp.concatenate([v[half:], v[:half]])
    scale = (nope + rope) ** -0.5
    gq_head = jnp.concatenate([q_norm_g, swap(q_norm_g[nope:])]) * (scale * LOG2E)
    gq = jnp.tile(gq_head, hh).reshape(hh * qp, 1)
    gkn = jnp.tile(jnp.concatenate([k_norm_g[:nope], jnp.zeros((2 * rope,), F32)]), hh).reshape(1, hh * qp)
    gkr = jnp.concatenate([jnp.zeros((nope,), F32), k_norm_g[nope:], swap(k_norm_g[nope:])]).reshape(1, qp)
    inv = ROPE_BASE ** (-jnp.arange(0, rope, 2, dtype=F32) / rope)
    ang = jnp.arange(s, dtype=F32)[:, None] * inv[None, :]
    cos, sin = jnp.cos(ang), jnp.sin(ang)
    cs = jnp.concatenate([cos, cos, sin, sin], axis=1)
    csk = jnp.concatenate([jnp.zeros((s, nope), F32), cs], axis=1)
    gmat = _group_mean_matrix([nope, rope, rope], 2 * qp)
    feat_major = lambda w: pl.BlockSpec((1, 1, w, t), lambda bi, i: (bi, i, 0, 0))
    qt, k, vt = pl.pallas_call(
        _mla_proj_kernel,
        grid=(b, nb),
        in_specs=[pl.BlockSpec((1, t, d), lambda bi, i: (bi, i, 0)),
                  _const_spec((1, d)),
                  _const_spec((d, MLA_Q_LORA)),
                  _const_spec((d, MLA_KV_LORA)),
                  _const_spec((d, qp)),
                  _const_spec((1, MLA_Q_LORA)),
                  _const_spec((1, MLA_KV_LORA)),
                  _const_spec((hh * qp, MLA_Q_LORA)),
                  _const_spec((MLA_KV_LORA, hh * qp)),
                  _const_spec((hh * dv, MLA_KV_LORA)),
                  _const_spec((2 * qp, 2 * qp)),
                  _const_spec((hh * qp, 1)),
                  pl.BlockSpec((2 * rope, t), lambda bi, i: (0, i)),
                  _const_spec((1, hh * qp)),
                  _const_spec((1, qp)),
                  pl.BlockSpec((t, qp), lambda bi, i: (i, 0))],
        out_specs=[feat_major(hh * qp),
                   pl.BlockSpec((1, t, hh * qp), lambda bi, i: (bi, i, 0)),
                   feat_major(hh * dv)],
        out_shape=[jax.ShapeDtypeStruct((b, nb, hh * qp, t), BF16),
                   jax.ShapeDtypeStruct((b, s, hh * qp), BF16),
                   jax.ShapeDtypeStruct((b, nb, hh * dv, t), BF16)],
        compiler_params=_params(("parallel", "parallel")),
        name="mla_proj",
    )(x3, g.reshape(1, d), wa_q, wa_kv, war, q_a_norm.reshape(1, -1), kv_a_norm.reshape(1, -1),
      wuqt, wuk, wuvt, gmat, gq, cs.T, gkn, gkr, csk)

    o = pl.pallas_call(
        _mla_attn_kernel,
        grid=(b, hh // 2, nb),
        in_specs=[pl.BlockSpec((1, 1, 2 * qp, t), lambda bi, h, i: (bi, i, h, 0)),
                  pl.BlockSpec((1, s, 2 * qp), lambda bi, h, i: (bi, 0, h)),
                  pl.BlockSpec((1, nb, 2 * dv, t), lambda bi, h, i: (bi, 0, h, 0))],
        out_specs=pl.BlockSpec((1, t, 2 * dv), lambda bi, h, i: (bi, i, h)),
        out_shape=jax.ShapeDtypeStruct((b, s, hh * dv), BF16),
        scratch_shapes=[pltpu.VMEM((2, 1, t), F32), pltpu.VMEM((2, 1, t), F32),
                        pltpu.VMEM((2, dv, t), F32),
                        pltpu.VMEM((2, 2, t, t), F32),
                        pltpu.VMEM((2, 2, 1, t), F32)],
        compiler_params=_params(("parallel", "parallel", "arbitrary")),
        name="mla_attn",
    )(qt, k, vt)
    return o.reshape(b * s, hh * dv), w_out


def kernel(x, rel_bias_table, ffn_norm, ffn_w_in, ffn_w_out, mixer_norm, conv_w_in, conv_b_in, conv_w_dw, conv_b_dw, conv_ln_g, conv_ln_b, conv_w_out, conv_b_out, diff_w_in, diff_q_norm, diff_k_norm, diff_lambda, diff_sub_norm, diff_w_out, sb_w_in, sb_w_out, mla_w_a, mla_q_a_norm, mla_kv_a_norm, mla_w_uq, mla_w_ukv, mla_q_norm, mla_k_norm, mla_w_out):
    b, s, d = x.shape
    assert d == D_MODEL and s % TOK_TILE == 0
    depth = ffn_norm.shape[0]
    ffn = lambda xx, i, k, proj=None: _ffn(xx.reshape(b * s, d), ffn_norm[i, k], ffn_w_in, ffn_w_out,
                                           i, k, proj).reshape(b, s, d)
    for i in range(depth):
        mixer, j = i % N_MIXERS, i // N_MIXERS
        x = ffn(x, i, 0)
        g = mixer_norm[i]
        proj = None
        if mixer == 0:
            x = _conv_mixer(x, g, conv_w_in[j], conv_b_in[j], conv_w_dw[j], conv_b_dw[j],
                            conv_ln_g[j], conv_ln_b[j], conv_w_out[j], conv_b_out[j])
        elif mixer == 1:
            proj = _diff_mixer(x, i, g, diff_w_in[j], diff_q_norm[j], diff_k_norm[j], diff_lambda[j],
                               diff_sub_norm[j], diff_w_out[j], rel_bias_table)
        elif mixer == 2:
            proj = _sb_mixer(x, g, sb_w_in[j], sb_w_out[j])
        else:
            proj = _mla_mixer(x, g, mla_w_a[j], mla_q_a_norm[j], mla_kv_a_norm[j], mla_w_uq[j],
                              mla_w_ukv[j], mla_q_norm[j], mla_k_norm[j], mla_w_out[j])
        x = ffn(x, i, 1, proj)
    return x
```

```python
import functools
import math

import jax
import jax.numpy as jnp
from jax import lax
from jax.experimental import pallas as pl
from jax.experimental.pallas import tpu as pltpu

F32 = jnp.float32
BF16 = jnp.bfloat16

D_MODEL = 1024
DEPTH = 4
CHUNK = 64
N_MIXERS = 4
D_FF = 2816
RMS_EPS = 1e-6
LN_EPS = 1e-5
MASK_VALUE = -1e30
CONV_WIDTH = 31
DIFF_HEADS = 8
DIFF_QK_DIM = 64
DIFF_V_DIM = 128
LAMBDA_INIT_BASE = 0.8
LAMBDA_INIT_SCALE = 0.6
LAMBDA_INIT_DECAY = 0.3
REL_BUCKETS = 32
REL_MAX_DIST = 128
SB_HEADS = 16
SB_HEAD_DIM = 64
MLA_HEADS = 16
MLA_Q_LORA = 384
MLA_KV_LORA = 256
MLA_NOPE = 64
MLA_ROPE = 32
MLA_V = 64
ROPE_BASE = 10000.0
LOG2E = math.log2(math.e)

LANES = 128
SUBLANES = 8
TOK_TILE = 512
FF_CHUNK = 256
FAR_UNROLL = 4
SB_KEY_BLOCK = 128
SB_SWEEP_KEYS = 256
CONV_HALO = 32
CONV_ROWS = 16
VMEM_LIMIT = 56 * 1024 * 1024
F32_EXP2_ZERO = -151.0


def _params(sem):
    return pltpu.CompilerParams(dimension_semantics=sem, vmem_limit_bytes=VMEM_LIMIT)


def _const_spec(shape):
    nd = len(shape)
    return pl.BlockSpec(shape, lambda *_: (0,) * nd, pipeline_mode=pl.Buffered(1))


def _rms_bf16(x, g):
    ms = jnp.mean(x * x, axis=-1, keepdims=True)
    return (x * lax.rsqrt(ms + RMS_EPS) * g).astype(BF16)


def _split_dot(x, w):
    hi = x.astype(BF16)
    lo = (x - hi.astype(F32)).astype(BF16)
    return (jnp.dot(hi, w, preferred_element_type=F32)
            + jnp.dot(lo, w, preferred_element_type=F32))


def _dot_nt(a, b):
    return lax.dot_general(a, b, (((1,), (1,)), ((), ())), preferred_element_type=F32)


def _rows_rms(y, g_col):
    ms = jnp.mean(y * y, axis=0, keepdims=True)
    return y * lax.rsqrt(ms + RMS_EPS) * g_col


def _ffn_kernel(*refs, fused_proj):
    if fused_proj:
        x_ref, a_ref, wa_ref, g_ref, win_ref, wout_ref, o_ref, gate_ref = refs
        x = x_ref[...] + jnp.dot(a_ref[...], wa_ref[...], preferred_element_type=F32)
    else:
        x_ref, g_ref, win_ref, wout_ref, o_ref, gate_ref = refs
        x = x_ref[...]
    h = _rms_bf16(x, g_ref[...])
    for c in range(D_FF // FF_CHUNK):
        lo, hi = c * FF_CHUNK, (c + 1) * FF_CHUNK
        a = jnp.dot(h, win_ref[:, lo:hi].astype(BF16), preferred_element_type=F32)
        u = jnp.dot(h, win_ref[:, D_FF + lo:D_FF + hi].astype(BF16), preferred_element_type=F32)
        gate_ref[:, lo:hi] = (a * jax.nn.sigmoid(a) * u).astype(BF16)
    y = jnp.dot(gate_ref[...], wout_ref[...].astype(BF16), preferred_element_type=F32)
    o_ref[...] = x + 0.5 * y


def _ffn(x2, g, w_in_all, w_out_all, layer, half, proj=None):
    n, d = x2.shape
    tm = TOK_TILE
    rows = lambda w: pl.BlockSpec((tm, w), lambda i: (i, 0))
    pick = lambda r, c: pl.BlockSpec((None, None, r, c), lambda i: (layer, half, 0, 0),
                                     pipeline_mode=pl.Buffered(1))
    in_specs, args = [rows(d)], [x2]
    if proj is not None:
        a, w_a = proj
        in_specs += [rows(a.shape[1]), _const_spec(w_a.shape)]
        args += [a, w_a.astype(BF16)]
    in_specs += [_const_spec((1, d)), pick(d, 2 * D_FF), pick(D_FF, d)]
    args += [g.reshape(1, d), w_in_all, w_out_all]
    return pl.pallas_call(
        functools.partial(_ffn_kernel, fused_proj=proj is not None),
        grid=(n // tm,),
        in_specs=in_specs,
        out_specs=rows(d),
        out_shape=jax.ShapeDtypeStruct((n, d), F32),
        scratch_shapes=[pltpu.VMEM((tm, D_FF), BF16)],
        compiler_params=_params(("parallel",)),
        name="ffn_proj" if proj is not None else "ffn",
    )(*args)


def _conv_in_kernel(x_ref, g_ref, w_ref, b_ref, u_ref):
    d = D_MODEL
    h = _rms_bf16(x_ref[...], g_ref[...])
    y = jnp.dot(h, w_ref[...], preferred_element_type=F32) + b_ref[...]
    u_ref[...] = y[:, :d] * jax.nn.sigmoid(y[:, d:])


def _conv_out_kernel(x_ref, ucur_ref, uprev_ref, wdw_ref, bdw_ref, lng_ref, lnb_ref,
                     wout_ref, bout_ref, y_ref, ext_ref, act_ref):
    ts = ucur_ref.shape[1]
    i = pl.program_id(1)
    ext_ref[0, 0:CONV_HALO, :] = jnp.where(i > 0, uprev_ref[0], 0.0)
    ext_ref[0, CONV_HALO:, :] = ucur_ref[0]
    n_shift = ts + CONV_HALO - SUBLANES
    for p in range(1, SUBLANES):
        ext_ref[p, 0:n_shift, :] = ext_ref[0, p:p + n_shift, :]
    off = CONV_HALO - (CONV_WIDTH - 1)
    for r in range(ts // CONV_ROWS):
        r0 = r * CONV_ROWS
        acc = None
        for k in range(CONV_WIDTH):
            p = (off + k) % SUBLANES
            a = r0 + off + k - p
            win = ext_ref[p, a:a + CONV_ROWS, :].reshape(CONV_ROWS // SUBLANES, SUBLANES, -1)
            tap = win * wdw_ref[k]
            acc = tap if acc is None else acc + tap
        c = acc.reshape(CONV_ROWS, -1) + bdw_ref[...]
        mu = jnp.mean(c, axis=-1, keepdims=True)
        cc = c - mu
        var = jnp.mean(cc * cc, axis=-1, keepdims=True)
        ln = cc * lax.rsqrt(var + LN_EPS) * lng_ref[...] + lnb_ref[...]
        act_ref[r0:r0 + CONV_ROWS, :] = (ln * jax.nn.sigmoid(ln)).astype(BF16)
    y = jnp.dot(act_ref[...], wout_ref[...], preferred_element_type=F32) + bout_ref[...]
    y_ref[0] = x_ref[0] + y


def _conv_mixer(x3, g, w_in, b_in, w_dw, b_dw, ln_g, ln_b, w_out, b_out):
    b, s, d = x3.shape
    n = b * s
    tm = TOK_TILE
    u = pl.pallas_call(
        _conv_in_kernel,
        grid=(n // tm,),
        in_specs=[pl.BlockSpec((tm, d), lambda i: (i, 0)),
                  _const_spec((1, d)),
                  _const_spec((d, 2 * d)),
                  _const_spec((1, 2 * d))],
        out_specs=pl.BlockSpec((tm, d), lambda i: (i, 0)),
        out_shape=jax.ShapeDtypeStruct((n, d), F32),
        compiler_params=_params(("parallel",)),
        name="conv_in",
    )(x3.reshape(n, d), g.reshape(1, d), w_in.astype(BF16), b_in.reshape(1, 2 * d))
    u3 = u.reshape(b, s, d)
    halo_per_tile = tm // CONV_HALO
    row = lambda v: v.reshape(1, d)
    return pl.pallas_call(
        _conv_out_kernel,
        grid=(b, s // tm),
        in_specs=[pl.BlockSpec((1, tm, d), lambda bi, i: (bi, i, 0)),
                  pl.BlockSpec((1, tm, d), lambda bi, i: (bi, i, 0)),
                  pl.BlockSpec((1, CONV_HALO, d),
                               lambda bi, i: (bi, jnp.maximum(i * halo_per_tile - 1, 0), 0)),
                  _const_spec((CONV_WIDTH, SUBLANES, d)),
                  _const_spec((1, d)), _const_spec((1, d)), _const_spec((1, d)),
                  _const_spec((d, d)),
                  _const_spec((1, d))],
        out_specs=pl.BlockSpec((1, tm, d), lambda bi, i: (bi, i, 0)),
        out_shape=jax.ShapeDtypeStruct((b, s, d), F32),
        scratch_shapes=[pltpu.VMEM((SUBLANES, tm + CONV_HALO, d), F32), pltpu.VMEM((tm, d), BF16)],
        compiler_params=_params(("parallel", "parallel")),
        name="conv_out",
    )(x3, u3, u3, jnp.broadcast_to(w_dw[:, None, :], (CONV_WIDTH, SUBLANES, d)), row(b_dw),
      row(ln_g), row(ln_b), w_out.astype(BF16), row(b_out))


def _stage_logits(s, s_ref, mb_ref):
    s_ref[...] = s
    mb_ref[...] = jnp.max(s, axis=0, keepdims=True)


def _softmax_step_t(s_ref, mb_ref, vt, m_ref, l_ref, acc_ref):
    m_prev = m_ref[...]
    m_new = jnp.maximum(m_prev, mb_ref[...])
    alpha = jnp.exp2(m_prev - m_new)
    p = jnp.exp2(s_ref[...] - m_new)
    l_ref[...] = alpha * l_ref[...] + jnp.sum(p, axis=0, keepdims=True)
    acc_ref[...] = alpha * acc_ref[...] + jnp.dot(vt, p.astype(BF16), preferred_element_type=F32)
    m_ref[...] = m_new


FAR, PREV, DIAG = "far", "prev", "diag"


def _sweep_key_tiles(i, logits, absorb):
    def pipelined(tiles):
        logits(tiles[0][0], tiles[0][1], 0)
        for n, (j, _) in enumerate(tiles):
            if n + 1 < len(tiles):
                logits(tiles[n + 1][0], tiles[n + 1][1], (n + 1) % 2)
            absorb(j, n % 2)

    n_far = jnp.maximum(i - 1, 0)
    n_groups = n_far // FAR_UNROLL

    def far_group(jg, carry):
        pipelined([(jg * FAR_UNROLL + u, FAR) for u in range(FAR_UNROLL)])
        return carry

    lax.fori_loop(0, n_groups, far_group, 0)
    rest = n_groups * FAR_UNROLL

    for left in range(FAR_UNROLL):
        @pl.when(jnp.logical_and(i >= 1, n_far - rest == left))
        def _tail(left=left):
            pipelined([(rest + u, FAR) for u in range(left)] + [(i - 1, PREV), (i, DIAG)])

    @pl.when(i == 0)
    def _only_diag():
        pipelined([(i, DIAG)])


def _group_mean_matrix(groups, width):
    idx = []
    for gi, size in enumerate(groups):
        idx += [gi] * size
    reps = width // len(idx)
    gid = jnp.asarray([r * len(groups) + g for r in range(reps) for g in idx], jnp.int32)
    sizes = jnp.asarray([float(groups[g]) for _ in range(reps) for g in idx], F32)
    same = gid[:, None] == gid[None, :]
    return jnp.where(same, 1.0 / sizes[None, :], 0.0).astype(BF16)


def _diff_proj_kernel(x_ref, g_ref, wqt_ref, wk_ref, wvt_ref, gq_ref, gk_ref, gmat_ref,
                      qt_ref, k_ref, vt_ref):
    h = _rms_bf16(x_ref[0], g_ref[...])
    yqt = _dot_nt(wqt_ref[...], h)
    for r in range(yqt.shape[0] // DIFF_QK_DIM):
        sl = slice(r * DIFF_QK_DIM, (r + 1) * DIFF_QK_DIM)
        qt_ref[0, 0, sl, :] = _rows_rms(yqt[sl, :], gq_ref[sl, :]).astype(BF16)
    yk = jnp.dot(h, wk_ref[...], preferred_element_type=F32)
    width = gmat_ref.shape[0]
    for c in range(yk.shape[1] // width):
        sl = slice(c * width, (c + 1) * width)
        y = yk[:, sl]
        ms = _split_dot(y * y, gmat_ref[...])
        k_ref[0, :, sl] = (y * lax.rsqrt(ms + RMS_EPS) * gk_ref[:, sl]).astype(BF16)
    vt_ref[0, 0] = _dot_nt(wvt_ref[...], h).astype(BF16)


def _diff_attn_kernel(tab_ref, scal_ref, qt_ref, k_ref, vt_ref, bucket_ref, subg_ref, o_ref,
                      bias_ref, m_ref, l_ref, acc_ref, s_ref, mb_ref):
    t = qt_ref.shape[3]
    hd = pl.program_id(0)
    i = pl.program_id(2)

    @pl.when(jnp.logical_and(pl.program_id(1) == 0, i == 0))
    def _build_bias():
        far = tab_ref[REL_BUCKETS // 2 - 1, hd]
        keys = lax.broadcasted_iota(jnp.int32, (t, t), 0)
        queries = lax.broadcasted_iota(jnp.int32, (t, t), 1)
        visible = (keys // CHUNK) <= (queries // CHUNK)
        for tile in range(2):
            bk = bucket_ref[tile]
            bias = jnp.zeros((t, t), F32)
            for b in range(REL_BUCKETS):
                bias = jnp.where(bk == b, (tab_ref[b, hd] - far) * LOG2E, bias)
            if tile == 1:
                bias = jnp.where(visible, bias, MASK_VALUE)
            bias_ref[tile] = bias

    m_ref[...] = jnp.full(m_ref.shape, -jnp.inf, F32)
    l_ref[...] = jnp.zeros(l_ref.shape, F32)
    acc_ref[...] = jnp.zeros(acc_ref.shape, F32)

    qt = qt_ref[0, 0]
    feat = lax.broadcasted_iota(jnp.int32, qt.shape, 0)
    qmaps = [jnp.where((feat // DIFF_QK_DIM) == mp, qt, jnp.zeros_like(qt)) for mp in range(2)]

    bias_tile = {FAR: None, PREV: 0, DIAG: 1}

    def logits(j, kind, slot):
        k = k_ref[0, pl.ds(pl.multiple_of(j * t, t), t), :]
        ss = [jnp.dot(k, qmaps[mp], preferred_element_type=F32) for mp in range(2)]
        for mp in range(2):
            s = ss[mp] if kind == FAR else ss[mp] + bias_ref[bias_tile[kind]]
            _stage_logits(s, s_ref.at[slot, mp], mb_ref.at[slot, mp])

    def absorb(j, slot):
        vt = vt_ref[0, j]
        for mp in range(2):
            _softmax_step_t(s_ref.at[slot, mp], mb_ref.at[slot, mp], vt,
                            m_ref.at[mp], l_ref.at[mp], acc_ref.at[mp])

    _sweep_key_tiles(i, logits, absorb)

    lam = scal_ref[0]
    attn = acc_ref[0] / l_ref[0] - lam * (acc_ref[1] / l_ref[1])
    o_ref[0] = _rows_rms(attn, subg_ref[...]).T.astype(BF16)


def _t5_bucket(rel):
    nb = REL_BUCKETS // 2
    bucket = jnp.where(rel > 0, nb, 0)
    n = jnp.abs(rel)
    max_exact = nb // 2
    n_f = jnp.maximum(n, 1).astype(jnp.float32)
    large = max_exact + (jnp.log(n_f / max_exact) / math.log(REL_MAX_DIST / max_exact)
                         * (nb - max_exact)).astype(jnp.int32)
    large = jnp.minimum(large, nb - 1)
    return bucket + jnp.where(n < max_exact, n, large)


def _diff_mixer(x3, layer_idx, g, w_in, q_norm_g, k_norm_g, lam, sub_norm_g, w_out, rel_table):
    b, s, d = x3.shape
    t = TOK_TILE
    nb = s // t
    hh, dq, dv = DIFF_HEADS, DIFF_QK_DIM, DIFF_V_DIM
    w3 = w_in.reshape(d, hh, 4 * dq + dv)
    wqt = w3[:, :, :2 * dq].reshape(d, hh * 2 * dq).T.astype(BF16)
    wk = w3[:, :, 2 * dq:4 * dq].reshape(d, hh * 2 * dq).astype(BF16)
    wvt = w3[:, :, 4 * dq:].reshape(d, hh * dv).T.astype(BF16)
    scale = dq ** -0.5
    gq = (jnp.tile(q_norm_g, 2 * hh) * (scale * LOG2E)).reshape(hh * 2 * dq, 1)
    gk = jnp.tile(k_norm_g, 2 * hh).reshape(1, hh * 2 * dq)
    gmat = _group_mean_matrix([dq], 2 * LANES)
    feat_major = lambda w: pl.BlockSpec((1, 1, w, t), lambda bi, i: (bi, i, 0, 0))
    qt, k, vt = pl.pallas_call(
        _diff_proj_kernel,
        grid=(b, nb),
        in_specs=[pl.BlockSpec((1, t, d), lambda bi, i: (bi, i, 0)),
                  _const_spec((1, d)),
                  _const_spec((hh * 2 * dq, d)),
                  _const_spec((d, hh * 2 * dq)),
                  _const_spec((hh * dv, d)),
                  _const_spec((hh * 2 * dq, 1)),
                  _const_spec((1, hh * 2 * dq)),
                  _const_spec((2 * LANES, 2 * LANES))],
        out_specs=[feat_major(hh * 2 * dq),
                   pl.BlockSpec((1, t, hh * 2 * dq), lambda bi, i: (bi, i, 0)),
                   feat_major(hh * dv)],
        out_shape=[jax.ShapeDtypeStruct((b, nb, hh * 2 * dq, t), BF16),
                   jax.ShapeDtypeStruct((b, s, hh * 2 * dq), BF16),
                   jax.ShapeDtypeStruct((b, nb, hh * dv, t), BF16)],
        compiler_params=_params(("parallel", "parallel")),
        name="diff_proj",
    )(x3, g.reshape(1, d), wqt, wk, wvt, gq, gk, gmat)

    lam_init = LAMBDA_INIT_BASE - LAMBDA_INIT_SCALE * math.exp(-LAMBDA_INIT_DECAY * layer_idx)
    lam_full = (jnp.exp(jnp.sum(lam[0] * lam[1]).astype(F32))
                - jnp.exp(jnp.sum(lam[2] * lam[3]).astype(F32)) + lam_init)
    scal = jnp.reshape(lam_full, (1,)).astype(F32)
    r = jnp.arange(t)
    rel_diag = r[:, None] - r[None, :]
    buckets = jnp.stack([_t5_bucket(rel_diag - t), _t5_bucket(rel_diag)]).astype(jnp.int32)
    subg = (sub_norm_g * (1.0 - lam_init)).reshape(dv, 1)
    smem = pl.BlockSpec(memory_space=pltpu.SMEM)
    o = pl.pallas_call(
        _diff_attn_kernel,
        grid=(hh, b, nb),
        in_specs=[smem, smem,
                  pl.BlockSpec((1, 1, 2 * dq, t), lambda h, bi, i: (bi, i, h, 0)),
                  pl.BlockSpec((1, s, 2 * dq), lambda h, bi, i: (bi, 0, h)),
                  pl.BlockSpec((1, nb, dv, t), lambda h, bi, i: (bi, 0, h, 0)),
                  _const_spec((2, t, t)),
                  _const_spec((dv, 1))],
        out_specs=pl.BlockSpec((1, t, dv), lambda h, bi, i: (bi, i, h)),
        out_shape=jax.ShapeDtypeStruct((b, s, hh * dv), BF16),
        scratch_shapes=[pltpu.VMEM((2, t, t), F32),
                        pltpu.VMEM((2, 1, t), F32),
                        pltpu.VMEM((2, 1, t), F32),
                        pltpu.VMEM((2, dv, t), F32),
                        pltpu.VMEM((2, 2, t, t), F32),
                        pltpu.VMEM((2, 2, 1, t), F32)],
        compiler_params=_params(("parallel", "arbitrary", "arbitrary")),
        name="diff_attn",
    )(rel_table.astype(F32), scal, qt, k, vt, buckets, subg)
    return o.reshape(b * s, hh * dv), w_out


def _sb_proj_kernel(x_ref, g_ref, wqt_ref, wk_ref, wvt_ref, qt_ref, k_ref, vt_ref):
    h = _rms_bf16(x_ref[0], g_ref[...])
    qt_ref[0, 0] = (_dot_nt(wqt_ref[...], h) * (SB_HEAD_DIM ** -0.5 * LOG2E)).astype(BF16)
    k_ref[0] = jnp.dot(h, wk_ref[...], preferred_element_type=F32).astype(BF16)
    yvt = _dot_nt(wvt_ref[...], h).astype(BF16)
    sk = vt_ref.shape[3]
    for c in range(vt_ref.shape[1]):
        vt_ref[0, c] = yvt[:, c * sk:(c + 1) * sk]


def _sb_attn_kernel(qt_ref, k_ref, vt_ref, tri_ref, o_ref, run_ref, acc_ref, base_ref, tot_ref):
    t = qt_ref.shape[3]
    kb = SB_KEY_BLOCK
    sk = SB_SWEEP_KEYS
    dh = SB_HEAD_DIM
    i = pl.program_id(2)
    sub = sk // kb
    qt = qt_ref[0, 0]
    feat = lax.broadcasted_iota(jnp.int32, qt.shape, 0)
    qh = [jnp.where((feat // dh) == a, qt, jnp.zeros_like(qt)) for a in range(2)]
    run_ref[...] = jnp.zeros(run_ref.shape, F32)
    acc_ref[...] = jnp.zeros(acc_ref.shape, F32)

    def logits(j, c0):
        k = k_ref[0, pl.ds(pl.multiple_of(j * sk, sk), sk), :]
        return [jnp.dot(k, qh[a][:, c0:], preferred_element_type=F32) for a in range(2)]

    def causal_mask(c0):
        shape = (sk, t - c0)
        return lax.broadcasted_iota(jnp.int32, shape, 0) < lax.broadcasted_iota(jnp.int32, shape, 1)

    def prepare(zs, diagonal, c0, slot):
        qs = slice(c0, t)
        for a in range(2):
            z = zs[a]
            soft = jnp.log2(1.0 + jnp.exp2(-jnp.abs(z)))
            log_beta = jnp.minimum(z, 0.0) - soft
            log_keep = log_beta - z
            if diagonal:
                log_keep = jnp.where(causal_mask(c0), log_keep, 0.0)
            hi = log_keep.astype(BF16)
            lo = (log_keep - hi.astype(F32)).astype(BF16)
            for blk in range(sub):
                rows = slice(blk * kb, (blk + 1) * kb)
                sums = jnp.dot(tri_ref[...], jnp.concatenate([hi[rows], lo[rows]], axis=0),
                               preferred_element_type=F32)
                base_ref[slot, a, rows, qs] = log_beta[rows] + sums[:kb]
                tot_ref[slot, a, blk, :, qs] = sums[kb:]

    def fold(j, diagonal, c0, slot):
        qs = slice(c0, t)
        vt = vt_ref[0, j]
        for a in range(2):
            run = run_ref[a, :, qs]
            ws = [None] * sub
            for blk in range(sub - 1, -1, -1):
                rows = slice(blk * kb, (blk + 1) * kb)
                ws[blk] = jnp.exp2(base_ref[slot, a, rows, qs] + run[0:1])
                run = run + tot_ref[slot, a, blk, :, qs]
            w = jnp.concatenate(ws, axis=0)
            if diagonal:
                w = jnp.where(causal_mask(c0), w, 0.0)
            acc_ref[a, :, qs] += jnp.dot(vt[a * dh:(a + 1) * dh, :], w.astype(BF16),
                                         preferred_element_type=F32)
            run_ref[a, :, qs] = run

    per_tile = t // sk
    own = [(i * per_tile + b, True, b * sk) for b in range(per_tile - 1, -1, -1)]

    def run_blocks(blocks):
        zs = logits(blocks[0][0], blocks[0][2])
        for n, (j, diagonal, c0) in enumerate(blocks):
            nxt = logits(blocks[n + 1][0], blocks[n + 1][2]) if n + 1 < len(blocks) else None
            prepare(zs, diagonal, c0, n)
            if n >= 1:
                fold(blocks[n - 1][0], blocks[n - 1][1], blocks[n - 1][2], n - 1)
            zs = nxt
        last = len(blocks) - 1
        fold(blocks[last][0], blocks[last][1], blocks[last][2], last)

    @pl.when(i >= 1)
    def _with_previous():
        run_blocks(own + [(i * per_tile - 1, False, 0)])

    @pl.when(i == 0)
    def _first_tile():
        run_blocks(own)

    def cond(c):
        j, top = c
        return jnp.logical_and(j >= 0, top > F32_EXP2_ZERO)

    def body(c):
        j, _ = c
        prepare(logits(j, 0), False, 0, 0)
        fold(j, False, 0, 0)
        return j - 1, jnp.max(run_ref[...])

    lax.while_loop(cond, body, (i * per_tile - 2, jnp.max(run_ref[...])))
    o_ref[0] = jnp.concatenate([acc_ref[0], acc_ref[1]], axis=0).T.astype(BF16)


def _sb_mixer(x3, g, w_in, w_out):
    b, s, d = x3.shape
    t = TOK_TILE
    kb = SB_KEY_BLOCK
    sweep = SB_SWEEP_KEYS
    nb = s // t
    hh, dh = SB_HEADS, SB_HEAD_DIM
    w3 = w_in.reshape(d, hh, 3 * dh)
    wqt = w3[:, :, :dh].reshape(d, hh * dh).T.astype(BF16)
    wk = w3[:, :, dh:2 * dh].reshape(d, hh * dh).astype(BF16)
    wvt = w3[:, :, 2 * dh:].reshape(d, hh * dh).T.astype(BF16)
    qt, k, vt = pl.pallas_call(
        _sb_proj_kernel,
        grid=(b, nb),
        in_specs=[pl.BlockSpec((1, t, d), lambda bi, i: (bi, i, 0)),
                  _const_spec((1, d)),
                  _const_spec((hh * dh, d)),
                  _const_spec((d, hh * dh)),
                  _const_spec((hh * dh, d))],
        out_specs=[pl.BlockSpec((1, 1, hh * dh, t), lambda bi, i: (bi, i, 0, 0)),
                   pl.BlockSpec((1, t, hh * dh), lambda bi, i: (bi, i, 0)),
                   pl.BlockSpec((1, t // sweep, hh * dh, sweep), lambda bi, i: (bi, i, 0, 0))],
        out_shape=[jax.ShapeDtypeStruct((b, nb, hh * dh, t), BF16),
                   jax.ShapeDtypeStruct((b, s, hh * dh), BF16),
                   jax.ShapeDtypeStruct((b, s // sweep, hh * dh, sweep), BF16)],
        compiler_params=_params(("parallel", "parallel")),
        name="sb_proj",
    )(x3, g.reshape(1, d), wqt, wk, wvt)

    rj = jnp.arange(kb + SUBLANES)
    sk = jnp.arange(2 * kb) % kb
    tri = jnp.where((rj[:, None] >= kb) | (sk[None, :] > rj[:, None]), 1.0, 0.0).astype(BF16)
    pair = 2 * dh
    o = pl.pallas_call(
        _sb_attn_kernel,
        grid=(b, hh // 2, nb),
        in_specs=[pl.BlockSpec((1, 1, pair, t), lambda bi, h, i: (bi, i, h, 0)),
                  pl.BlockSpec((1, s, pair), lambda bi, h, i: (bi, 0, h)),
                  pl.BlockSpec((1, s // sweep, pair, sweep), lambda bi, h, i: (bi, 0, h, 0)),
                  _const_spec((kb + SUBLANES, 2 * kb))],
        out_specs=pl.BlockSpec((1, t, pair), lambda bi, h, i: (bi, i, h)),
        out_shape=jax.ShapeDtypeStruct((b, s, hh * dh), BF16),
        scratch_shapes=[pltpu.VMEM((2, SUBLANES, t), F32), pltpu.VMEM((2, dh, t), F32),
                        pltpu.VMEM((t // sweep + 1, 2, sweep, t), F32),
                        pltpu.VMEM((t // sweep + 1, 2, sweep // kb, SUBLANES, t), F32)],
        compiler_params=_params(("parallel", "parallel", "arbitrary")),
        name="sb_attn",
    )(qt, k, vt, tri)
    return o.reshape(b * s, hh * dh), w_out


MLA_QK_PAD = MLA_NOPE + 2 * MLA_ROPE


def _mla_proj_kernel(x_ref, g_ref, waq_ref, wakv_ref, war_ref, gqa_ref, gkva_ref, wuqt_ref,
                     wuk_ref, wuvt_ref, gmat_ref, gq_ref, cst_ref, gkn_ref, gkr_ref, csk_ref,
                     qt_ref, k_ref, vt_ref):
    nope, rope, qp = MLA_NOPE, MLA_ROPE, MLA_QK_PAD
    h = _rms_bf16(x_ref[0], g_ref[...])
    cq = _rms_bf16(jnp.dot(h, waq_ref[...], preferred_element_type=F32), gqa_ref[...])
    ckv = _rms_bf16(jnp.dot(h, wakv_ref[...], preferred_element_type=F32), gkva_ref[...])
    yqt = _dot_nt(wuqt_ref[...], cq)
    cs = cst_ref[...]
    for hd in range(MLA_HEADS):
        base = hd * qp
        y = yqt[base:base + qp, :]
        gcol = gq_ref[base:base + qp, :]
        qt_ref[0, 0, base:base + nope, :] = _rows_rms(y[:nope], gcol[:nope]).astype(BF16)
        msr = jnp.mean(y[nope:nope + rope] * y[nope:nope + rope], axis=0, keepdims=True)
        rr = y[nope:] * lax.rsqrt(msr + RMS_EPS) * gcol[nope:] * cs
        qf = (rr[:rope] + rr[rope:]).astype(BF16)
        qt_ref[0, 0, base + nope:base + nope + rope, :] = qf
        qt_ref[0, 0, base + nope + rope:base + qp, :] = qf
    width = gmat_ref.shape[0]
    gm = gmat_ref[...]
    sh = jnp.dot(h, war_ref[...], preferred_element_type=F32)
    ms = _split_dot(sh * sh, gm[:qp, :qp])
    shn = sh * lax.rsqrt(ms + RMS_EPS) * gkr_ref[...] * csk_ref[...]
    shared = jnp.concatenate([shn] * (width // qp), axis=1)
    ykn = jnp.dot(ckv, wuk_ref[...], preferred_element_type=F32)
    for c in range(ykn.shape[1] // width):
        sl = slice(c * width, (c + 1) * width)
        y = ykn[:, sl]
        ms = _split_dot(y * y, gm)
        k_ref[0, :, sl] = (y * lax.rsqrt(ms + RMS_EPS) * gkn_ref[:, sl] + shared).astype(BF16)
    vt_ref[0, 0] = _dot_nt(wuvt_ref[...], ckv).astype(BF16)


def _mla_attn_kernel(qt_ref, k_ref, vt_ref, o_ref, m_ref, l_ref, acc_ref, s_ref, mb_ref):
    t = qt_ref.shape[3]
    qp, dv = MLA_QK_PAD, MLA_V
    i = pl.program_id(2)
    m_ref[...] = jnp.full(m_ref.shape, -jnp.inf, F32)
    l_ref[...] = jnp.zeros(l_ref.shape, F32)
    acc_ref[...] = jnp.zeros(acc_ref.shape, F32)

    def logits(j, kind, slot):
        k = k_ref[0, pl.ds(pl.multiple_of(j * t, t), t), :]
        ss = [jnp.dot(k[:, a * qp:(a + 1) * qp], qt_ref[0, 0, a * qp:(a + 1) * qp, :],
                      preferred_element_type=F32) for a in range(2)]
        for a in range(2):
            s = ss[a]
            if kind == DIAG:
                keys = lax.broadcasted_iota(jnp.int32, (t, t), 0)
                queries = lax.broadcasted_iota(jnp.int32, (t, t), 1)
                s = jnp.where((keys // CHUNK) <= (queries // CHUNK), s, MASK_VALUE)
            _stage_logits(s, s_ref.at[slot, a], mb_ref.at[slot, a])

    def absorb(j, slot):
        vt = vt_ref[0, j]
        for a in range(2):
            _softmax_step_t(s_ref.at[slot, a], mb_ref.at[slot, a], vt[a * dv:(a + 1) * dv, :],
                            m_ref.at[a], l_ref.at[a], acc_ref.at[a])

    _sweep_key_tiles(i, logits, absorb)
    o = jnp.concatenate([acc_ref[0] / l_ref[0], acc_ref[1] / l_ref[1]], axis=0)
    o_ref[0] = o.T.astype(BF16)


def _rotate_half_cols(w):
    half = MLA_ROPE // 2
    return jnp.concatenate([-w[..., half:], w[..., :half]], axis=-1)


def _mla_mixer(x3, g, w_a, q_a_norm, kv_a_norm, w_uq, w_ukv, q_norm_g, k_norm_g, w_out):
    b, s, d = x3.shape
    t = TOK_TILE
    nb = s // t
    hh = MLA_HEADS
    nope, rope, qp, dv = MLA_NOPE, MLA_ROPE, MLA_QK_PAD, MLA_V
    half = rope // 2
    wa_q = w_a[:, :MLA_Q_LORA].astype(BF16)
    wa_kv = w_a[:, MLA_Q_LORA:MLA_Q_LORA + MLA_KV_LORA].astype(BF16)
    wa_r = w_a[:, MLA_Q_LORA + MLA_KV_LORA:]
    war = jnp.concatenate([jnp.zeros((d, nope), F32), wa_r, _rotate_half_cols(wa_r)], axis=1).astype(BF16)
    wq3 = w_uq.reshape(MLA_Q_LORA, hh, nope + rope)
    wuqt = jnp.concatenate([wq3, _rotate_half_cols(wq3[:, :, nope:])], axis=2
                           ).reshape(MLA_Q_LORA, hh * qp).T.astype(BF16)
    wkv3 = w_ukv.reshape(MLA_KV_LORA, hh, nope + dv)
    wuk = jnp.concatenate([wkv3[:, :, :nope], jnp.zeros((MLA_KV_LORA, hh, 2 * rope), F32)], axis=2
                          ).reshape(MLA_KV_LORA, hh * qp).astype(BF16)
    wuvt = wkv3[:, :, nope:].reshape(MLA_KV_LORA, hh * dv).T.astype(BF16)
    swap = lambda v: jnp.concatenate([v[half:], v[:half]])
    scale = (nope + rope) ** -0.5
    gq_head = jnp.concatenate([q_norm_g, swap(q_norm_g[nope:])]) * (scale * LOG2E)
    gq = jnp.tile(gq_head, hh).reshape(hh * qp, 1)
    gkn = jnp.tile(jnp.concatenate([k_norm_g[:nope], jnp.zeros((2 * rope,), F32)]), hh).reshape(1, hh * qp)
    gkr = jnp.concatenate([jnp.zeros((nope,), F32), k_norm_g[nope:], swap(k_norm_g[nope:])]).reshape(1, qp)
    inv = ROPE_BASE ** (-jnp.arange(0, rope, 2, dtype=F32) / rope)
    ang = jnp.arange(s, dtype=F32)[:, None] * inv[None, :]
    cos, sin = jnp.cos(ang), jnp.sin(ang)
    cs = jnp.concatenate([cos, cos, sin, sin], axis=1)
    csk = jnp.concatenate([jnp.zeros((s, nope), F32), cs], axis=1)
    gmat = _group_mean_matrix([nope, rope, rope], 2 * qp)
    feat_major = lambda w: pl.BlockSpec((1, 1, w, t), lambda bi, i: (bi, i, 0, 0))
    qt, k, vt = pl.pallas_call(
        _mla_proj_kernel,
        grid=(b, nb),
        in_specs=[pl.BlockSpec((1, t, d), lambda bi, i: (bi, i, 0)),
                  _const_spec((1, d)),
                  _const_spec((d, MLA_Q_LORA)),
                  _const_spec((d, MLA_KV_LORA)),
                  _const_spec((d, qp)),
                  _const_spec((1, MLA_Q_LORA)),
                  _const_spec((1, MLA_KV_LORA)),
                  _const_spec((hh * qp, MLA_Q_LORA)),
                  _const_spec((MLA_KV_LORA, hh * qp)),
                  _const_spec((hh * dv, MLA_KV_LORA)),
                  _const_spec((2 * qp, 2 * qp)),
                  _const_spec((hh * qp, 1)),
                  pl.BlockSpec((2 * rope, t), lambda bi, i: (0, i)),
                  _const_spec((1, hh * qp)),
                  _const_spec((1, qp)),
                  pl.BlockSpec((t, qp), lambda bi, i: (i, 0))],
        out_specs=[feat_major(hh * qp),
                   pl.BlockSpec((1, t, hh * qp), lambda bi, i: (bi, i, 0)),
                   feat_major(hh * dv)],
        out_shape=[jax.ShapeDtypeStruct((b, nb, hh * qp, t), BF16),
                   jax.ShapeDtypeStruct((b, s, hh * qp), BF16),
                   jax.ShapeDtypeStruct((b, nb, hh * dv, t), BF16)],
        compiler_params=_params(("parallel", "parallel")),
        name="mla_proj",
    )(x3, g.reshape(1, d), wa_q, wa_kv, war, q_a_norm.reshape(1, -1), kv_a_norm.reshape(1, -1),
      wuqt, wuk, wuvt, gmat, gq, cs.T, gkn, gkr, csk)

    o = pl.pallas_call(
        _mla_attn_kernel,
        grid=(b, hh // 2, nb),
        in_specs=[pl.BlockSpec((1, 1, 2 * qp, t), lambda bi, h, i: (bi, i, h, 0)),
                  pl.BlockSpec((1, s, 2 * qp), lambda bi, h, i: (bi, 0, h)),
                  pl.BlockSpec((1, nb, 2 * dv, t), lambda bi, h, i: (bi, 0, h, 0))],
        out_specs=pl.BlockSpec((1, t, 2 * dv), lambda bi, h, i: (bi, i, h)),
        out_shape=jax.ShapeDtypeStruct((b, s, hh * dv), BF16),
        scratch_shapes=[pltpu.VMEM((2, 1, t), F32), pltpu.VMEM((2, 1, t), F32),
                        pltpu.VMEM((2, dv, t), F32),
                        pltpu.VMEM((2, 2, t, t), F32),
                        pltpu.VMEM((2, 2, 1, t), F32)],
        compiler_params=_params(("parallel", "parallel", "arbitrary")),
        name="mla_attn",
    )(qt, k, vt)
    return o.reshape(b * s, hh * dv), w_out


def kernel(x, rel_bias_table, ffn_norm, ffn_w_in, ffn_w_out, mixer_norm, conv_w_in, conv_b_in, conv_w_dw, conv_b_dw, conv_ln_g, conv_ln_b, conv_w_out, conv_b_out, diff_w_in, diff_q_norm, diff_k_norm, diff_lambda, diff_sub_norm, diff_w_out, sb_w_in, sb_w_out, mla_w_a, mla_q_a_norm, mla_kv_a_norm, mla_w_uq, mla_w_ukv, mla_q_norm, mla_k_norm, mla_w_out):
    b, s, d = x.shape
    assert d == D_MODEL and s % TOK_TILE == 0
    depth = ffn_norm.shape[0]
    ffn = lambda xx, i, k, proj=None: _ffn(xx.reshape(b * s, d), ffn_norm[i, k], ffn_w_in, ffn_w_out,
                                           i, k, proj).reshape(b, s, d)
    for i in range(depth):
        mixer, j = i % N_MIXERS, i // N_MIXERS
        x = ffn(x, i, 0)
        g = mixer_norm[i]
        proj = None
        if mixer == 0:
            x = _conv_mixer(x, g, conv_w_in[j], conv_b_in[j], conv_w_dw[j], conv_b_dw[j],
                            conv_ln_g[j], conv_ln_b[j], conv_w_out[j], conv_b_out[j])
        elif mixer == 1:
            proj = _diff_mixer(x, i, g, diff_w_in[j], diff_q_norm[j], diff_k_norm[j], diff_lambda[j],
                               diff_sub_norm[j], diff_w_out[j], rel_bias_table)
        elif mixer == 2:
            proj = _sb_mixer(x, g, sb_w_in[j], sb_w_out[j])
        else:
            proj = _mla_mixer(x, g, mla_w_a[j], mla_q_a_norm[j], mla_kv_a_norm[j], mla_w_uq[j],
                              mla_w_ukv[j], mla_q_norm[j], mla_k_norm[j], mla_w_out[j])
        x = ffn(x, i, 1, proj)
    return x
```

```python
import functools
import math

import jax
import jax.numpy as jnp
from jax import lax
from jax.experimental import pallas as pl
from jax.experimental.pallas import tpu as pltpu

F32 = jnp.float32
BF16 = jnp.bfloat16

D_MODEL = 1024
DEPTH = 4
CHUNK = 64
N_MIXERS = 4
D_FF = 2816
RMS_EPS = 1e-6
LN_EPS = 1e-5
MASK_VALUE = -1e30
CONV_WIDTH = 31
DIFF_HEADS = 8
DIFF_QK_DIM = 64
DIFF_V_DIM = 128
LAMBDA_INIT_BASE = 0.8
LAMBDA_INIT_SCALE = 0.6
LAMBDA_INIT_DECAY = 0.3
REL_BUCKETS = 32
REL_MAX_DIST = 128
SB_HEADS = 16
SB_HEAD_DIM = 64
MLA_HEADS = 16
MLA_Q_LORA = 384
MLA_KV_LORA = 256
MLA_NOPE = 64
MLA_ROPE = 32
MLA_V = 64
ROPE_BASE = 10000.0
LOG2E = math.log2(math.e)

LANES = 128
SUBLANES = 8
TOK_TILE = 512
FF_CHUNK = 256
FAR_UNROLL = 4
SB_KEY_BLOCK = 128
SB_SWEEP_KEYS = 256
CONV_HALO = 32
CONV_ROWS = 16
VMEM_LIMIT = 56 * 1024 * 1024
F32_EXP2_ZERO = -151.0


def _params(sem):
    return pltpu.CompilerParams(dimension_semantics=sem, vmem_limit_bytes=VMEM_LIMIT)


def _const_spec(shape):
    nd = len(shape)
    return pl.BlockSpec(shape, lambda *_: (0,) * nd, pipeline_mode=pl.Buffered(1))


def _rms_bf16(x, g):
    ms = jnp.mean(x * x, axis=-1, keepdims=True)
    return (x * lax.rsqrt(ms + RMS_EPS) * g).astype(BF16)


def _split_dot(x, w):
    hi = x.astype(BF16)
    lo = (x - hi.astype(F32)).astype(BF16)
    return (jnp.dot(hi, w, preferred_element_type=F32)
            + jnp.dot(lo, w, preferred_element_type=F32))


def _dot_nt(a, b):
    return lax.dot_general(a, b, (((1,), (1,)), ((), ())), preferred_element_type=F32)


def _rows_rms(y, g_col):
    ms = jnp.mean(y * y, axis=0, keepdims=True)
    return y * lax.rsqrt(ms + RMS_EPS) * g_col


def _ffn_kernel(*refs, fused_proj):
    if fused_proj:
        x_ref, a_ref, wa_ref, g_ref, win_ref, wout_ref, o_ref, gate_ref = refs
        x = x_ref[...] + jnp.dot(a_ref[...], wa_ref[...], preferred_element_type=F32)
    else:
        x_ref, g_ref, win_ref, wout_ref, o_ref, gate_ref = refs
        x = x_ref[...]
    h = _rms_bf16(x, g_ref[...])
    for c in range(D_FF // FF_CHUNK):
        lo, hi = c * FF_CHUNK, (c + 1) * FF_CHUNK
        a = jnp.dot(h, win_ref[:, lo:hi].astype(BF16), preferred_element_type=F32)
        u = jnp.dot(h, win_ref[:, D_FF + lo:D_FF + hi].astype(BF16), preferred_element_type=F32)
        gate_ref[:, lo:hi] = (a * jax.nn.sigmoid(a) * u).astype(BF16)
    y = jnp.dot(gate_ref[...], wout_ref[...].astype(BF16), preferred_element_type=F32)
    o_ref[...] = x + 0.5 * y


def _ffn(x2, g, w_in_all, w_out_all, layer, half, proj=None):
    n, d = x2.shape
    tm = TOK_TILE
    rows = lambda w: pl.BlockSpec((tm, w), lambda i: (i, 0))
    pick = lambda r, c: pl.BlockSpec((None, None, r, c), lambda i: (layer, half, 0, 0),
                                     pipeline_mode=pl.Buffered(1))
    in_specs, args = [rows(d)], [x2]
    if proj is not None:
        a, w_a = proj
        in_specs += [rows(a.shape[1]), _const_spec(w_a.shape)]
        args += [a, w_a.astype(BF16)]
    in_specs += [_const_spec((1, d)), pick(d, 2 * D_FF), pick(D_FF, d)]
    args += [g.reshape(1, d), w_in_all, w_out_all]
    return pl.pallas_call(
        functools.partial(_ffn_kernel, fused_proj=proj is not None),
        grid=(n // tm,),
        in_specs=in_specs,
        out_specs=rows(d),
        out_shape=jax.ShapeDtypeStruct((n, d), F32),
        scratch_shapes=[pltpu.VMEM((tm, D_FF), BF16)],
        compiler_params=_params(("parallel",)),
        name="ffn_proj" if proj is not None else "ffn",
    )(*args)


def _conv_in_kernel(x_ref, g_ref, w_ref, b_ref, u_ref):
    d = D_MODEL
    h = _rms_bf16(x_ref[...], g_ref[...])
    y = jnp.dot(h, w_ref[...], preferred_element_type=F32) + b_ref[...]
    u_ref[...] = y[:, :d] * jax.nn.sigmoid(y[:, d:])


def _conv_out_kernel(x_ref, ucur_ref, uprev_ref, wdw_ref, bdw_ref, lng_ref, lnb_ref,
                     wout_ref, bout_ref, y_ref, ext_ref, act_ref):
    ts = ucur_ref.shape[1]
    i = pl.program_id(1)
    ext_ref[0, 0:CONV_HALO, :] = jnp.where(i > 0, uprev_ref[0], 0.0)
    ext_ref[0, CONV_HALO:, :] = ucur_ref[0]
    n_shift = ts + CONV_HALO - SUBLANES
    for p in range(1, SUBLANES):
        ext_ref[p, 0:n_shift, :] = ext_ref[0, p:p + n_shift, :]
    off = CONV_HALO - (CONV_WIDTH - 1)
    for r in range(ts // CONV_ROWS):
        r0 = r * CONV_ROWS
        acc = None
        for k in range(CONV_WIDTH):
            p = (off + k) % SUBLANES
            a = r0 + off + k - p
            win = ext_ref[p, a:a + CONV_ROWS, :].reshape(CONV_ROWS // SUBLANES, SUBLANES, -1)
            tap = win * wdw_ref[k]
            acc = tap if acc is None else acc + tap
        c = acc.reshape(CONV_ROWS, -1) + bdw_ref[...]
        mu = jnp.mean(c, axis=-1, keepdims=True)
        cc = c - mu
        var = jnp.mean(cc * cc, axis=-1, keepdims=True)
        ln = cc * lax.rsqrt(var + LN_EPS) * lng_ref[...] + lnb_ref[...]
        act_ref[r0:r0 + CONV_ROWS, :] = (ln * jax.nn.sigmoid(ln)).astype(BF16)
    y = jnp.dot(act_ref[...], wout_ref[...], preferred_element_type=F32) + bout_ref[...]
    y_ref[0] = x_ref[0] + y


def _conv_mixer(x3, g, w_in, b_in, w_dw, b_dw, ln_g, ln_b, w_out, b_out):
    b, s, d = x3.shape
    n = b * s
    tm = TOK_TILE
    u = pl.pallas_call(
        _conv_in_kernel,
        grid=(n // tm,),
        in_specs=[pl.BlockSpec((tm, d), lambda i: (i, 0)),
                  _const_spec((1, d)),
                  _const_spec((d, 2 * d)),
                  _const_spec((1, 2 * d))],
        out_specs=pl.BlockSpec((tm, d), lambda i: (i, 0)),
        out_shape=jax.ShapeDtypeStruct((n, d), F32),
        compiler_params=_params(("parallel",)),
        name="conv_in",
    )(x3.reshape(n, d), g.reshape(1, d), w_in.astype(BF16), b_in.reshape(1, 2 * d))
    u3 = u.reshape(b, s, d)
    halo_per_tile = tm // CONV_HALO
    row = lambda v: v.reshape(1, d)
    return pl.pallas_call(
        _conv_out_kernel,
        grid=(b, s // tm),
        in_specs=[pl.BlockSpec((1, tm, d), lambda bi, i: (bi, i, 0)),
                  pl.BlockSpec((1, tm, d), lambda bi, i: (bi, i, 0)),
                  pl.BlockSpec((1, CONV_HALO, d),
                               lambda bi, i: (bi, jnp.maximum(i * halo_per_tile - 1, 0), 0)),
                  _const_spec((CONV_WIDTH, SUBLANES, d)),
                  _const_spec((1, d)), _const_spec((1, d)), _const_spec((1, d)),
                  _const_spec((d, d)),
                  _const_spec((1, d))],
        out_specs=pl.BlockSpec((1, tm, d), lambda bi, i: (bi, i, 0)),
        out_shape=jax.ShapeDtypeStruct((b, s, d), F32),
        scratch_shapes=[pltpu.VMEM((SUBLANES, tm + CONV_HALO, d), F32), pltpu.VMEM((tm, d), BF16)],
        compiler_params=_params(("parallel", "parallel")),
        name="conv_out",
    )(x3, u3, u3, jnp.broadcast_to(w_dw[:, None, :], (CONV_WIDTH, SUBLANES, d)), row(b_dw),
      row(ln_g), row(ln_b), w_out.astype(BF16), row(b_out))


def _stage_logits(s, s_ref, mb_ref):
    s_ref[...] = s
    mb_ref[...] = jnp.max(s, axis=0, keepdims=True)


def _softmax_step_t(s_ref, mb_ref, vt, m_ref, l_ref, acc_ref):
    m_prev = m_ref[...]
    m_new = jnp.maximum(m_prev, mb_ref[...])
    alpha = jnp.exp2(m_prev - m_new)
    p = jnp.exp2(s_ref[...] - m_new)
    l_ref[...] = alpha * l_ref[...] + jnp.sum(p, axis=0, keepdims=True)
    acc_ref[...] = alpha * acc_ref[...] + jnp.dot(vt, p.astype(BF16), preferred_element_type=F32)
    m_ref[...] = m_new


FAR, PREV, DIAG = "far", "prev", "diag"


def _sweep_key_tiles(i, logits, absorb):
    def pipelined(tiles):
        logits(tiles[0][0], tiles[0][1], 0)
        for n, (j, _) in enumerate(tiles):
            if n + 1 < len(tiles):
                logits(tiles[n + 1][0], tiles[n + 1][1], (n + 1) % 2)
            absorb(j, n % 2)

    n_far = jnp.maximum(i - 1, 0)
    n_groups = n_far // FAR_UNROLL

    def far_group(jg, carry):
        pipelined([(jg * FAR_UNROLL + u, FAR) for u in range(FAR_UNROLL)])
        return carry

    lax.fori_loop(0, n_groups, far_group, 0)
    rest = n_groups * FAR_UNROLL

    for left in range(FAR_UNROLL):
        @pl.when(jnp.logical_and(i >= 1, n_far - rest == left))
        def _tail(left=left):
            pipelined([(rest + u, FAR) for u in range(left)] + [(i - 1, PREV), (i, DIAG)])

    @pl.when(i == 0)
    def _only_diag():
        pipelined([(i, DIAG)])


def _group_mean_matrix(groups, width):
    idx = []
    for gi, size in enumerate(groups):
        idx += [gi] * size
    reps = width // len(idx)
    gid = jnp.asarray([r * len(groups) + g for r in range(reps) for g in idx], jnp.int32)
    sizes = jnp.asarray([float(groups[g]) for _ in range(reps) for g in idx], F32)
    same = gid[:, None] == gid[None, :]
    return jnp.where(same, 1.0 / sizes[None, :], 0.0).astype(BF16)


def _diff_proj_kernel(x_ref, g_ref, wqt_ref, wk_ref, wvt_ref, gq_ref, gk_ref, gmat_ref,
                      qt_ref, k_ref, vt_ref):
    h = _rms_bf16(x_ref[0], g_ref[...])
    yqt = _dot_nt(wqt_ref[...], h)
    for r in range(yqt.shape[0] // DIFF_QK_DIM):
        sl = slice(r * DIFF_QK_DIM, (r + 1) * DIFF_QK_DIM)
        qt_ref[0, 0, sl, :] = _rows_rms(yqt[sl, :], gq_ref[sl, :]).astype(BF16)
    yk = jnp.dot(h, wk_ref[...], preferred_element_type=F32)
    width = gmat_ref.shape[0]
    for c in range(yk.shape[1] // width):
        sl = slice(c * width, (c + 1) * width)
        y = yk[:, sl]
        ms = _split_dot(y * y, gmat_ref[...])
        k_ref[0, :, sl] = (y * lax.rsqrt(ms + RMS_EPS) * gk_ref[:, sl]).astype(BF16)
    vt_ref[0, 0] = _dot_nt(wvt_ref[...], h).astype(BF16)


def _diff_attn_kernel(tab_ref, scal_ref, qt_ref, k_ref, vt_ref, bucket_ref, subg_ref, o_ref,
                      bias_ref, m_ref, l_ref, acc_ref, s_ref, mb_ref):
    t = qt_ref.shape[3]
    hd = pl.program_id(0)
    i = pl.program_id(2)

    @pl.when(jnp.logical_and(pl.program_id(1) == 0, i == 0))
    def _build_bias():
        far = tab_ref[REL_BUCKETS // 2 - 1, hd]
        keys = lax.broadcasted_iota(jnp.int32, (t, t), 0)
        queries = lax.broadcasted_iota(jnp.int32, (t, t), 1)
        visible = (keys // CHUNK) <= (queries // CHUNK)
        for tile in range(2):
            bk = bucket_ref[tile]
            bias = jnp.zeros((t, t), F32)
            for b in range(REL_BUCKETS):
                bias = jnp.where(bk == b, (tab_ref[b, hd] - far) * LOG2E, bias)
            if tile == 1:
                bias = jnp.where(visible, bias, MASK_VALUE)
            bias_ref[tile] = bias

    m_ref[...] = jnp.full(m_ref.shape, -jnp.inf, F32)
    l_ref[...] = jnp.zeros(l_ref.shape, F32)
    acc_ref[...] = jnp.zeros(acc_ref.shape, F32)

    qt = qt_ref[0, 0]
    feat = lax.broadcasted_iota(jnp.int32, qt.shape, 0)
    qmaps = [jnp.where((feat // DIFF_QK_DIM) == mp, qt, jnp.zeros_like(qt)) for mp in range(2)]

    bias_tile = {FAR: None, PREV: 0, DIAG: 1}

    def logits(j, kind, slot):
        k = k_ref[0, pl.ds(pl.multiple_of(j * t, t), t), :]
        ss = [jnp.dot(k, qmaps[mp], preferred_element_type=F32) for mp in range(2)]
        for mp in range(2):
            s = ss[mp] if kind == FAR else ss[mp] + bias_ref[bias_tile[kind]]
            _stage_logits(s, s_ref.at[slot, mp], mb_ref.at[slot, mp])

    def absorb(j, slot):
        vt = vt_ref[0, j]
        for mp in range(2):
            _softmax_step_t(s_ref.at[slot, mp], mb_ref.at[slot, mp], vt,
                            m_ref.at[mp], l_ref.at[mp], acc_ref.at[mp])

    _sweep_key_tiles(i, logits, absorb)

    lam = scal_ref[0]
    attn = acc_ref[0] / l_ref[0] - lam * (acc_ref[1] / l_ref[1])
    o_ref[0] = _rows_rms(attn, subg_ref[...]).T.astype(BF16)


def _t5_bucket(rel):
    nb = REL_BUCKETS // 2
    bucket = jnp.where(rel > 0, nb, 0)
    n = jnp.abs(rel)
    max_exact = nb // 2
    n_f = jnp.maximum(n, 1).astype(jnp.float32)
    large = max_exact + (jnp.log(n_f / max_exact) / math.log(REL_MAX_DIST / max_exact)
                         * (nb - max_exact)).astype(jnp.int32)
    large = jnp.minimum(large, nb - 1)
    return bucket + jnp.where(n < max_exact, n, large)


def _diff_mixer(x3, layer_idx, g, w_in, q_norm_g, k_norm_g, lam, sub_norm_g, w_out, rel_table):
    b, s, d = x3.shape
    t = TOK_TILE
    nb = s // t
    hh, dq, dv = DIFF_HEADS, DIFF_QK_DIM, DIFF_V_DIM
    w3 = w_in.reshape(d, hh, 4 * dq + dv)
    wqt = w3[:, :, :2 * dq].reshape(d, hh * 2 * dq).T.astype(BF16)
    wk = w3[:, :, 2 * dq:4 * dq].reshape(d, hh * 2 * dq).astype(BF16)
    wvt = w3[:, :, 4 * dq:].reshape(d, hh * dv).T.astype(BF16)
    scale = dq ** -0.5
    gq = (jnp.tile(q_norm_g, 2 * hh) * (scale * LOG2E)).reshape(hh * 2 * dq, 1)
    gk = jnp.tile(k_norm_g, 2 * hh).reshape(1, hh * 2 * dq)
    gmat = _group_mean_matrix([dq], 2 * LANES)
    feat_major = lambda w: pl.BlockSpec((1, 1, w, t), lambda bi, i: (bi, i, 0, 0))
    qt, k, vt = pl.pallas_call(
        _diff_proj_kernel,
        grid=(b, nb),
        in_specs=[pl.BlockSpec((1, t, d), lambda bi, i: (bi, i, 0)),
                  _const_spec((1, d)),
                  _const_spec((hh * 2 * dq, d)),
                  _const_spec((d, hh * 2 * dq)),
                  _const_spec((hh * dv, d)),
                  _const_spec((hh * 2 * dq, 1)),
                  _const_spec((1, hh * 2 * dq)),
                  _const_spec((2 * LANES, 2 * LANES))],
        out_specs=[feat_major(hh * 2 * dq),
                   pl.BlockSpec((1, t, hh * 2 * dq), lambda bi, i: (bi, i, 0)),
                   feat_major(hh * dv)],
        out_shape=[jax.ShapeDtypeStruct((b, nb, hh * 2 * dq, t), BF16),
                   jax.ShapeDtypeStruct((b, s, hh * 2 * dq), BF16),
                   jax.ShapeDtypeStruct((b, nb, hh * dv, t), BF16)],
        compiler_params=_params(("parallel", "parallel")),
        name="diff_proj",
    )(x3, g.reshape(1, d), wqt, wk, wvt, gq, gk, gmat)

    lam_init = LAMBDA_INIT_BASE - LAMBDA_INIT_SCALE * math.exp(-LAMBDA_INIT_DECAY * layer_idx)
    lam_full = (jnp.exp(jnp.sum(lam[0] * lam[1]).astype(F32))
                - jnp.exp(jnp.sum(lam[2] * lam[3]).astype(F32)) + lam_init)
    scal = jnp.reshape(lam_full, (1,)).astype(F32)
    r = jnp.arange(t)
    rel_diag = r[:, None] - r[None, :]
    buckets = jnp.stack([_t5_bucket(rel_diag - t), _t5_bucket(rel_diag)]).astype(jnp.int32)
    subg = (sub_norm_g * (1.0 - lam_init)).reshape(dv, 1)
    smem = pl.BlockSpec(memory_space=pltpu.SMEM)
    o = pl.pallas_call(
        _diff_attn_kernel,
        grid=(hh, b, nb),
        in_specs=[smem, smem,
                  pl.BlockSpec((1, 1, 2 * dq, t), lambda h, bi, i: (bi, i, h, 0)),
                  pl.BlockSpec((1, s, 2 * dq), lambda h, bi, i: (bi, 0, h)),
                  pl.BlockSpec((1, nb, dv, t), lambda h, bi, i: (bi, 0, h, 0)),
                  _const_spec((2, t, t)),
                  _const_spec((dv, 1))],
        out_specs=pl.BlockSpec((1, t, dv), lambda h, bi, i: (bi, i, h)),
        out_shape=jax.ShapeDtypeStruct((b, s, hh * dv), BF16),
        scratch_shapes=[pltpu.VMEM((2, t, t), F32),
                        pltpu.VMEM((2, 1, t), F32),
                        pltpu.VMEM((2, 1, t), F32),
                        pltpu.VMEM((2, dv, t), F32),
                        pltpu.VMEM((2, 2, t, t), F32),
                        pltpu.VMEM((2, 2, 1, t), F32)],
        compiler_params=_params(("parallel", "arbitrary", "arbitrary")),
        name="diff_attn",
    )(rel_table.astype(F32), scal, qt, k, vt, buckets, subg)
    return o.reshape(b * s, hh * dv), w_out


def _sb_proj_kernel(x_ref, g_ref, wqt_ref, wk_ref, wvt_ref, qt_ref, k_ref, vt_ref):
    h = _rms_bf16(x_ref[0], g_ref[...])
    qt_ref[0, 0] = (_dot_nt(wqt_ref[...], h) * (SB_HEAD_DIM ** -0.5 * LOG2E)).astype(BF16)
    k_ref[0] = jnp.dot(h, wk_ref[...], preferred_element_type=F32).astype(BF16)
    yvt = _dot_nt(wvt_ref[...], h).astype(BF16)
    sk = vt_ref.shape[3]
    for c in range(vt_ref.shape[1]):
        vt_ref[0, c] = yvt[:, c * sk:(c + 1) * sk]


def _sb_attn_kernel(qt_ref, k_ref, vt_ref, tri_ref, o_ref, run_ref, acc_ref, base_ref, tot_ref):
    t = qt_ref.shape[3]
    kb = SB_KEY_BLOCK
    sk = SB_SWEEP_KEYS
    dh = SB_HEAD_DIM
    i = pl.program_id(2)
    sub = sk // kb
    qt = qt_ref[0, 0]
    feat = lax.broadcasted_iota(jnp.int32, qt.shape, 0)
    qh = [jnp.where((feat // dh) == a, qt, jnp.zeros_like(qt)) for a in range(2)]
    run_ref[...] = jnp.zeros(run_ref.shape, F32)
    acc_ref[...] = jnp.zeros(acc_ref.shape, F32)

    def logits(j, lanes):
        k = k_ref[0, pl.ds(pl.multiple_of(j * sk, sk), sk), :]
        return [jnp.dot(k, qh[a][:, lanes[0]:lanes[1]], preferred_element_type=F32)
                for a in range(2)]

    def causal_mask(c0):
        shape = (sk, t - c0)
        return lax.broadcasted_iota(jnp.int32, shape, 0) < lax.broadcasted_iota(jnp.int32, shape, 1)

    def prepare(zs, diagonal, lanes, slot):
        c0 = lanes[0]
        qs = slice(*lanes)
        for a in range(2):
            z = zs[a]
            soft = jnp.log2(1.0 + jnp.exp2(-jnp.abs(z)))
            log_beta = jnp.minimum(z, 0.0) - soft
            log_keep = log_beta - z
            if diagonal:
                log_keep = jnp.where(causal_mask(c0), log_keep, 0.0)
            hi = log_keep.astype(BF16)
            lo = (log_keep - hi.astype(F32)).astype(BF16)
            for blk in range(sub):
                rows = slice(blk * kb, (blk + 1) * kb)
                sums = jnp.dot(tri_ref[...], jnp.concatenate([hi[rows], lo[rows]], axis=0),
                               preferred_element_type=F32)
                base_ref[slot, a, rows, qs] = log_beta[rows] + sums[:kb]
                tot_ref[slot, a, blk, :, qs] = sums[kb:]

    def fold(j, diagonal, lanes, slot):
        c0 = lanes[0]
        qs = slice(*lanes)
        vt = vt_ref[0, j]
        for a in range(2):
            run = run_ref[a, :, qs]
            ws = [None] * sub
            for blk in range(sub - 1, -1, -1):
                rows = slice(blk * kb, (blk + 1) * kb)
                ws[blk] = jnp.exp2(base_ref[slot, a, rows, qs] + run[0:1])
                run = run + tot_ref[slot, a, blk, :, qs]
            w = jnp.concatenate(ws, axis=0)
            if diagonal:
                w = jnp.where(causal_mask(c0), w, 0.0)
            acc_ref[a, :, qs] += jnp.dot(vt[a * dh:(a + 1) * dh, :], w.astype(BF16),
                                         preferred_element_type=F32)
            run_ref[a, :, qs] = run

    per_tile = t // sk
    own = [(i * per_tile + b, True, (b * sk, t)) for b in range(per_tile - 1, -1, -1)]

    def run_blocks(blocks):
        zs = logits(blocks[0][0], blocks[0][2])
        for n, (j, diagonal, lanes) in enumerate(blocks):
            nxt = logits(blocks[n + 1][0], blocks[n + 1][2]) if n + 1 < len(blocks) else None
            prepare(zs, diagonal, lanes, n)
            if n >= 1:
                fold(blocks[n - 1][0], blocks[n - 1][1], blocks[n - 1][2], n - 1)
            zs = nxt
        last = len(blocks) - 1
        fold(blocks[last][0], blocks[last][1], blocks[last][2], last)

    @pl.when(i >= 1)
    def _with_previous():
        run_blocks(own + [(i * per_tile - 1, False, (0, sk))])

    @pl.when(i == 0)
    def _first_tile():
        run_blocks(own)

    @pl.when(jnp.logical_and(i >= 1, jnp.max(run_ref[:, :, sk:]) > F32_EXP2_ZERO))
    def _previous_for_later_lanes():
        run_blocks([(i * per_tile - 1, False, (sk, t))])

    def cond(c):
        j, top = c
        return jnp.logical_and(j >= 0, top > F32_EXP2_ZERO)

    def body(c):
        j, _ = c
        run_blocks([(j, False, (0, t))])
        return j - 1, jnp.max(run_ref[...])

    lax.while_loop(cond, body, (i * per_tile - 2, jnp.max(run_ref[...])))
    o_ref[0] = jnp.concatenate([acc_ref[0], acc_ref[1]], axis=0).T.astype(BF16)


def _sb_mixer(x3, g, w_in, w_out):
    b, s, d = x3.shape
    t = TOK_TILE
    kb = SB_KEY_BLOCK
    sweep = SB_SWEEP_KEYS
    nb = s // t
    hh, dh = SB_HEADS, SB_HEAD_DIM
    w3 = w_in.reshape(d, hh, 3 * dh)
    wqt = w3[:, :, :dh].reshape(d, hh * dh).T.astype(BF16)
    wk = w3[:, :, dh:2 * dh].reshape(d, hh * dh).astype(BF16)
    wvt = w3[:, :, 2 * dh:].reshape(d, hh * dh).T.astype(BF16)
    qt, k, vt = pl.pallas_call(
        _sb_proj_kernel,
        grid=(b, nb),
        in_specs=[pl.BlockSpec((1, t, d), lambda bi, i: (bi, i, 0)),
                  _const_spec((1, d)),
                  _const_spec((hh * dh, d)),
                  _const_spec((d, hh * dh)),
                  _const_spec((hh * dh, d))],
        out_specs=[pl.BlockSpec((1, 1, hh * dh, t), lambda bi, i: (bi, i, 0, 0)),
                   pl.BlockSpec((1, t, hh * dh), lambda bi, i: (bi, i, 0)),
                   pl.BlockSpec((1, t // sweep, hh * dh, sweep), lambda bi, i: (bi, i, 0, 0))],
        out_shape=[jax.ShapeDtypeStruct((b, nb, hh * dh, t), BF16),
                   jax.ShapeDtypeStruct((b, s, hh * dh), BF16),
                   jax.ShapeDtypeStruct((b, s // sweep, hh * dh, sweep), BF16)],
        compiler_params=_params(("parallel", "parallel")),
        name="sb_proj",
    )(x3, g.reshape(1, d), wqt, wk, wvt)

    rj = jnp.arange(kb + SUBLANES)
    sk = jnp.arange(2 * kb) % kb
    tri = jnp.where((rj[:, None] >= kb) | (sk[None, :] > rj[:, None]), 1.0, 0.0).astype(BF16)
    pair = 2 * dh
    o = pl.pallas_call(
        _sb_attn_kernel,
        grid=(b, hh // 2, nb),
        in_specs=[pl.BlockSpec((1, 1, pair, t), lambda bi, h, i: (bi, i, h, 0)),
                  pl.BlockSpec((1, s, pair), lambda bi, h, i: (bi, 0, h)),
                  pl.BlockSpec((1, s // sweep, pair, sweep), lambda bi, h, i: (bi, 0, h, 0)),
                  _const_spec((kb + SUBLANES, 2 * kb))],
        out_specs=pl.BlockSpec((1, t, pair), lambda bi, h, i: (bi, i, h)),
        out_shape=jax.ShapeDtypeStruct((b, s, hh * dh), BF16),
        scratch_shapes=[pltpu.VMEM((2, SUBLANES, t), F32), pltpu.VMEM((2, dh, t), F32),
                        pltpu.VMEM((t // sweep + 1, 2, sweep, t), F32),
                        pltpu.VMEM((t // sweep + 1, 2, sweep // kb, SUBLANES, t), F32)],
        compiler_params=_params(("parallel", "parallel", "arbitrary")),
        name="sb_attn",
    )(qt, k, vt, tri)
    return o.reshape(b * s, hh * dh), w_out


MLA_QK_PAD = MLA_NOPE + 2 * MLA_ROPE


def _mla_proj_kernel(x_ref, g_ref, waq_ref, wakv_ref, war_ref, gqa_ref, gkva_ref, wuqt_ref,
                     wuk_ref, wuvt_ref, gmat_ref, gq_ref, cst_ref, gkn_ref, gkr_ref, csk_ref,
                     qt_ref, k_ref, vt_ref):
    nope, rope, qp = MLA_NOPE, MLA_ROPE, MLA_QK_PAD
    h = _rms_bf16(x_ref[0], g_ref[...])
    cq = _rms_bf16(jnp.dot(h, waq_ref[...], preferred_element_type=F32), gqa_ref[...])
    ckv = _rms_bf16(jnp.dot(h, wakv_ref[...], preferred_element_type=F32), gkva_ref[...])
    yqt = _dot_nt(wuqt_ref[...], cq)
    cs = cst_ref[...]
    for hd in range(MLA_HEADS):
        base = hd * qp
        y = yqt[base:base + qp, :]
        gcol = gq_ref[base:base + qp, :]
        qt_ref[0, 0, base:base + nope, :] = _rows_rms(y[:nope], gcol[:nope]).astype(BF16)
        msr = jnp.mean(y[nope:nope + rope] * y[nope:nope + rope], axis=0, keepdims=True)
        rr = y[nope:] * lax.rsqrt(msr + RMS_EPS) * gcol[nope:] * cs
        qf = (rr[:rope] + rr[rope:]).astype(BF16)
        qt_ref[0, 0, base + nope:base + nope + rope, :] = qf
        qt_ref[0, 0, base + nope + rope:base + qp, :] = qf
    width = gmat_ref.shape[0]
    gm = gmat_ref[...]
    sh = jnp.dot(h, war_ref[...], preferred_element_type=F32)
    ms = _split_dot(sh * sh, gm[:qp, :qp])
    shn = sh * lax.rsqrt(ms + RMS_EPS) * gkr_ref[...] * csk_ref[...]
    shared = jnp.concatenate([shn] * (width // qp), axis=1)
    ykn = jnp.dot(ckv, wuk_ref[...], preferred_element_type=F32)
    for c in range(ykn.shape[1] // width):
        sl = slice(c * width, (c + 1) * width)
        y = ykn[:, sl]
        ms = _split_dot(y * y, gm)
        k_ref[0, :, sl] = (y * lax.rsqrt(ms + RMS_EPS) * gkn_ref[:, sl] + shared).astype(BF16)
    vt_ref[0, 0] = _dot_nt(wuvt_ref[...], ckv).astype(BF16)


def _mla_attn_kernel(qt_ref, k_ref, vt_ref, o_ref, m_ref, l_ref, acc_ref, s_ref, mb_ref):
    t = qt_ref.shape[3]
    qp, dv = MLA_QK_PAD, MLA_V
    i = pl.program_id(2)
    m_ref[...] = jnp.full(m_ref.shape, -jnp.inf, F32)
    l_ref[...] = jnp.zeros(l_ref.shape, F32)
    acc_ref[...] = jnp.zeros(acc_ref.shape, F32)

    def logits(j, kind, slot):
        k = k_ref[0, pl.ds(pl.multiple_of(j * t, t), t), :]
        ss = [jnp.dot(k[:, a * qp:(a + 1) * qp], qt_ref[0, 0, a * qp:(a + 1) * qp, :],
                      preferred_element_type=F32) for a in range(2)]
        for a in range(2):
            s = ss[a]
            if kind == DIAG:
                keys = lax.broadcasted_iota(jnp.int32, (t, t), 0)
                queries = lax.broadcasted_iota(jnp.int32, (t, t), 1)
                s = jnp.where((keys // CHUNK) <= (queries // CHUNK), s, MASK_VALUE)
            _stage_logits(s, s_ref.at[slot, a], mb_ref.at[slot, a])

    def absorb(j, slot):
        vt = vt_ref[0, j]
        for a in range(2):
            _softmax_step_t(s_ref.at[slot, a], mb_ref.at[slot, a], vt[a * dv:(a + 1) * dv, :],
                            m_ref.at[a], l_ref.at[a], acc_ref.at[a])

    _sweep_key_tiles(i, logits, absorb)
    o = jnp.concatenate([acc_ref[0] / l_ref[0], acc_ref[1] / l_ref[1]], axis=0)
    o_ref[0] = o.T.astype(BF16)


def _rotate_half_cols(w):
    half = MLA_ROPE // 2
    return jnp.concatenate([-w[..., half:], w[..., :half]], axis=-1)


def _mla_mixer(x3, g, w_a, q_a_norm, kv_a_norm, w_uq, w_ukv, q_norm_g, k_norm_g, w_out):
    b, s, d = x3.shape
    t = TOK_TILE
    nb = s // t
    hh = MLA_HEADS
    nope, rope, qp, dv = MLA_NOPE, MLA_ROPE, MLA_QK_PAD, MLA_V
    half = rope // 2
    wa_q = w_a[:, :MLA_Q_LORA].astype(BF16)
    wa_kv = w_a[:, MLA_Q_LORA:MLA_Q_LORA + MLA_KV_LORA].astype(BF16)
    wa_r = w_a[:, MLA_Q_LORA + MLA_KV_LORA:]
    war = jnp.concatenate([jnp.zeros((d, nope), F32), wa_r, _rotate_half_cols(wa_r)], axis=1).astype(BF16)
    wq3 = w_uq.reshape(MLA_Q_LORA, hh, nope + rope)
    wuqt = jnp.concatenate([wq3, _rotate_half_cols(wq3[:, :, nope:])], axis=2
                           ).reshape(MLA_Q_LORA, hh * qp).T.astype(BF16)
    wkv3 = w_ukv.reshape(MLA_KV_LORA, hh, nope + dv)
    wuk = jnp.concatenate([wkv3[:, :, :nope], jnp.zeros((MLA_KV_LORA, hh, 2 * rope), F32)], axis=2
                          ).reshape(MLA_KV_LORA, hh * qp).astype(BF16)
    wuvt = wkv3[:, :, nope:].reshape(MLA_KV_LORA, hh * dv).T.astype(BF16)
    swap = lambda v: jnp.concatenate([v[half:], v[:half]])
    scale = (nope + rope) ** -0.5
    gq_head = jnp.concatenate([q_norm_g, swap(q_norm_g[nope:])]) * (scale * LOG2E)
    gq = jnp.tile(gq_head, hh).reshape(hh * qp, 1)
    gkn = jnp.tile(jnp.concatenate([k_norm_g[:nope], jnp.zeros((2 * rope,), F32)]), hh).reshape(1, hh * qp)
    gkr = jnp.concatenate([jnp.zeros((nope,), F32), k_norm_g[nope:], swap(k_norm_g[nope:])]).reshape(1, qp)
    inv = ROPE_BASE ** (-jnp.arange(0, rope, 2, dtype=F32) / rope)
    ang = jnp.arange(s, dtype=F32)[:, None] * inv[None, :]
    cos, sin = jnp.cos(ang), jnp.sin(ang)
    cs = jnp.concatenate([cos, cos, sin, sin], axis=1)
    csk = jnp.concatenate([jnp.zeros((s, nope), F32), cs], axis=1)
    gmat = _group_mean_matrix([nope, rope, rope], 2 * qp)
    feat_major = lambda w: pl.BlockSpec((1, 1, w, t), lambda bi, i: (bi, i, 0, 0))
    qt, k, vt = pl.pallas_call(
        _mla_proj_kernel,
        grid=(b, nb),
        in_specs=[pl.BlockSpec((1, t, d), lambda bi, i: (bi, i, 0)),
                  _const_spec((1, d)),
                  _const_spec((d, MLA_Q_LORA)),
                  _const_spec((d, MLA_KV_LORA)),
                  _const_spec((d, qp)),
                  _const_spec((1, MLA_Q_LORA)),
                  _const_spec((1, MLA_KV_LORA)),
                  _const_spec((hh * qp, MLA_Q_LORA)),
                  _const_spec((MLA_KV_LORA, hh * qp)),
                  _const_spec((hh * dv, MLA_KV_LORA)),
                  _const_spec((2 * qp, 2 * qp)),
                  _const_spec((hh * qp, 1)),
                  pl.BlockSpec((2 * rope, t), lambda bi, i: (0, i)),
                  _const_spec((1, hh * qp)),
                  _const_spec((1, qp)),
                  pl.BlockSpec((t, qp), lambda bi, i: (i, 0))],
        out_specs=[feat_major(hh * qp),
                   pl.BlockSpec((1, t, hh * qp), lambda bi, i: (bi, i, 0)),
                   feat_major(hh * dv)],
        out_shape=[jax.ShapeDtypeStruct((b, nb, hh * qp, t), BF16),
                   jax.ShapeDtypeStruct((b, s, hh * qp), BF16),
                   jax.ShapeDtypeStruct((b, nb, hh * dv, t), BF16)],
        compiler_params=_params(("parallel", "parallel")),
        name="mla_proj",
    )(x3, g.reshape(1, d), wa_q, wa_kv, war, q_a_norm.reshape(1, -1), kv_a_norm.reshape(1, -1),
      wuqt, wuk, wuvt, gmat, gq, cs.T, gkn, gkr, csk)

    o = pl.pallas_call(
        _mla_attn_kernel,
        grid=(b, hh // 2, nb),
        in_specs=[pl.BlockSpec((1, 1, 2 * qp, t), lambda bi, h, i: (bi, i, h, 0)),
                  pl.BlockSpec((1, s, 2 * qp), lambda bi, h, i: (bi, 0, h)),
                  pl.BlockSpec((1, nb, 2 * dv, t), lambda bi, h, i: (bi, 0, h, 0))],
        out_specs=pl.BlockSpec((1, t, 2 * dv), lambda bi, h, i: (bi, i, h)),
        out_shape=jax.ShapeDtypeStruct((b, s, hh * dv), BF16),
        scratch_shapes=[pltpu.VMEM((2, 1, t), F32), pltpu.VMEM((2, 1, t), F32),
                        pltpu.VMEM((2, dv, t), F32),
                        pltpu.VMEM((2, 2, t, t), F32),
                        pltpu.VMEM((2, 2, 1, t), F32)],
        compiler_params=_params(("parallel", "parallel", "arbitrary")),
        name="mla_attn",
    )(qt, k, vt)
    return o.reshape(b * s, hh * dv), w_out


def kernel(x, rel_bias_table, ffn_norm, ffn_w_in, ffn_w_out, mixer_norm, conv_w_in, conv_b_in, conv_w_dw, conv_b_dw, conv_ln_g, conv_ln_b, conv_w_out, conv_b_out, diff_w_in, diff_q_norm, diff_k_norm, diff_lambda, diff_sub_norm, diff_w_out, sb_w_in, sb_w_out, mla_w_a, mla_q_a_norm, mla_kv_a_norm, mla_w_uq, mla_w_ukv, mla_q_norm, mla_k_norm, mla_w_out):
    b, s, d = x.shape
    assert d == D_MODEL and s % TOK_TILE == 0
    depth = ffn_norm.shape[0]
    ffn = lambda xx, i, k, proj=None: _ffn(xx.reshape(b * s, d), ffn_norm[i, k], ffn_w_in, ffn_w_out,
                                           i, k, proj).reshape(b, s, d)
    for i in range(depth):
        mixer, j = i % N_MIXERS, i // N_MIXERS
        x = ffn(x, i, 0)
        g = mixer_norm[i]
        proj = None
        if mixer == 0:
            x = _conv_mixer(x, g, conv_w_in[j], conv_b_in[j], conv_w_dw[j], conv_b_dw[j],
                            conv_ln_g[j], conv_ln_b[j], conv_w_out[j], conv_b_out[j])
        elif mixer == 1:
            proj = _diff_mixer(x, i, g, diff_w_in[j], diff_q_norm[j], diff_k_norm[j], diff_lambda[j],
                               diff_sub_norm[j], diff_w_out[j], rel_bias_table)
        elif mixer == 2:
            proj = _sb_mixer(x, g, sb_w_in[j], sb_w_out[j])
        else:
            proj = _mla_mixer(x, g, mla_w_a[j], mla_q_a_norm[j], mla_kv_a_norm[j], mla_w_uq[j],
                              mla_w_ukv[j], mla_q_norm[j], mla_k_norm[j], mla_w_out[j])
        x = ffn(x, i, 1, proj)
    return x
```

```python
import functools
import math

import jax
import jax.numpy as jnp
from jax import lax
from jax.experimental import pallas as pl
from jax.experimental.pallas import tpu as pltpu

F32 = jnp.float32
BF16 = jnp.bfloat16

D_MODEL = 1024
DEPTH = 4
CHUNK = 64
N_MIXERS = 4
D_FF = 2816
RMS_EPS = 1e-6
LN_EPS = 1e-5
MASK_VALUE = -1e30
CONV_WIDTH = 31
DIFF_HEADS = 8
DIFF_QK_DIM = 64
DIFF_V_DIM = 128
LAMBDA_INIT_BASE = 0.8
LAMBDA_INIT_SCALE = 0.6
LAMBDA_INIT_DECAY = 0.3
REL_BUCKETS = 32
REL_MAX_DIST = 128
SB_HEADS = 16
SB_HEAD_DIM = 64
MLA_HEADS = 16
MLA_Q_LORA = 384
MLA_KV_LORA = 256
MLA_NOPE = 64
MLA_ROPE = 32
MLA_V = 64
ROPE_BASE = 10000.0
LOG2E = math.log2(math.e)

LANES = 128
SUBLANES = 8
TOK_TILE = 512
FF_CHUNK = 256
FAR_UNROLL = 4
SB_KEY_BLOCK = 128
SB_SWEEP_KEYS = 256
CONV_HALO = 32
CONV_ROWS = 16
VMEM_LIMIT = 56 * 1024 * 1024
F32_EXP2_ZERO = -151.0


def _params(sem):
    return pltpu.CompilerParams(dimension_semantics=sem, vmem_limit_bytes=VMEM_LIMIT)


def _const_spec(shape):
    nd = len(shape)
    return pl.BlockSpec(shape, lambda *_: (0,) * nd, pipeline_mode=pl.Buffered(1))


def _rms_bf16(x, g):
    ms = jnp.mean(x * x, axis=-1, keepdims=True)
    return (x * lax.rsqrt(ms + RMS_EPS) * g).astype(BF16)


def _split_dot(x, w):
    hi = x.astype(BF16)
    lo = (x - hi.astype(F32)).astype(BF16)
    return (jnp.dot(hi, w, preferred_element_type=F32)
            + jnp.dot(lo, w, preferred_element_type=F32))


def _dot_nt(a, b):
    return lax.dot_general(a, b, (((1,), (1,)), ((), ())), preferred_element_type=F32)


def _rows_rms(y, g_col):
    ms = jnp.mean(y * y, axis=0, keepdims=True)
    return y * lax.rsqrt(ms + RMS_EPS) * g_col


def _ffn_kernel(*refs, fused_proj):
    if fused_proj:
        x_ref, a_ref, wa_ref, g_ref, win_ref, wout_ref, o_ref, gate_ref = refs
        x = x_ref[...] + jnp.dot(a_ref[...], wa_ref[...], preferred_element_type=F32)
    else:
        x_ref, g_ref, win_ref, wout_ref, o_ref, gate_ref = refs
        x = x_ref[...]
    h = _rms_bf16(x, g_ref[...])
    for c in range(D_FF // FF_CHUNK):
        lo, hi = c * FF_CHUNK, (c + 1) * FF_CHUNK
        a = jnp.dot(h, win_ref[:, lo:hi].astype(BF16), preferred_element_type=F32)
        u = jnp.dot(h, win_ref[:, D_FF + lo:D_FF + hi].astype(BF16), preferred_element_type=F32)
        gate_ref[:, lo:hi] = (a * jax.nn.sigmoid(a) * u).astype(BF16)
    y = jnp.dot(gate_ref[...], wout_ref[...].astype(BF16), preferred_element_type=F32)
    o_ref[...] = x + 0.5 * y


def _ffn(x2, g, w_in_all, w_out_all, layer, half, proj=None):
    n, d = x2.shape
    tm = TOK_TILE
    rows = lambda w: pl.BlockSpec((tm, w), lambda i: (i, 0))
    pick = lambda r, c: pl.BlockSpec((None, None, r, c), lambda i: (layer, half, 0, 0),
                                     pipeline_mode=pl.Buffered(1))
    in_specs, args = [rows(d)], [x2]
    if proj is not None:
        a, w_a = proj
        in_specs += [rows(a.shape[1]), _const_spec(w_a.shape)]
        args += [a, w_a.astype(BF16)]
    in_specs += [_const_spec((1, d)), pick(d, 2 * D_FF), pick(D_FF, d)]
    args += [g.reshape(1, d), w_in_all, w_out_all]
    return pl.pallas_call(
        functools.partial(_ffn_kernel, fused_proj=proj is not None),
        grid=(n // tm,),
        in_specs=in_specs,
        out_specs=rows(d),
        out_shape=jax.ShapeDtypeStruct((n, d), F32),
        scratch_shapes=[pltpu.VMEM((tm, D_FF), BF16)],
        compiler_params=_params(("parallel",)),
        name="ffn_proj" if proj is not None else "ffn",
    )(*args)


def _conv_in_kernel(x_ref, g_ref, w_ref, b_ref, u_ref):
    d = D_MODEL
    h = _rms_bf16(x_ref[...], g_ref[...])
    y = jnp.dot(h, w_ref[...], preferred_element_type=F32) + b_ref[...]
    u_ref[...] = y[:, :d] * jax.nn.sigmoid(y[:, d:])


def _conv_out_kernel(x_ref, ucur_ref, uprev_ref, wdw_ref, bdw_ref, lng_ref, lnb_ref,
                     wout_ref, bout_ref, y_ref, ext_ref, act_ref):
    ts = ucur_ref.shape[1]
    i = pl.program_id(1)
    ext_ref[0, 0:CONV_HALO, :] = jnp.where(i > 0, uprev_ref[0], 0.0)
    ext_ref[0, CONV_HALO:, :] = ucur_ref[0]
    n_shift = ts + CONV_HALO - SUBLANES
    for p in range(1, SUBLANES):
        ext_ref[p, 0:n_shift, :] = ext_ref[0, p:p + n_shift, :]
    off = CONV_HALO - (CONV_WIDTH - 1)
    for r in range(ts // CONV_ROWS):
        r0 = r * CONV_ROWS
        acc = None
        for k in range(CONV_WIDTH):
            p = (off + k) % SUBLANES
            a = r0 + off + k - p
            win = ext_ref[p, a:a + CONV_ROWS, :].reshape(CONV_ROWS // SUBLANES, SUBLANES, -1)
            tap = win * wdw_ref[k]
            acc = tap if acc is None else acc + tap
        c = acc.reshape(CONV_ROWS, -1) + bdw_ref[...]
        mu = jnp.mean(c, axis=-1, keepdims=True)
        cc = c - mu
        var = jnp.mean(cc * cc, axis=-1, keepdims=True)
        ln = cc * lax.rsqrt(var + LN_EPS) * lng_ref[...] + lnb_ref[...]
        act_ref[r0:r0 + CONV_ROWS, :] = (ln * jax.nn.sigmoid(ln)).astype(BF16)
    y = jnp.dot(act_ref[...], wout_ref[...], preferred_element_type=F32) + bout_ref[...]
    y_ref[0] = x_ref[0] + y


def _conv_mixer(x3, g, w_in, b_in, w_dw, b_dw, ln_g, ln_b, w_out, b_out):
    b, s, d = x3.shape
    n = b * s
    tm = TOK_TILE
    u = pl.pallas_call(
        _conv_in_kernel,
        grid=(n // tm,),
        in_specs=[pl.BlockSpec((tm, d), lambda i: (i, 0)),
                  _const_spec((1, d)),
                  _const_spec((d, 2 * d)),
                  _const_spec((1, 2 * d))],
        out_specs=pl.BlockSpec((tm, d), lambda i: (i, 0)),
        out_shape=jax.ShapeDtypeStruct((n, d), F32),
        compiler_params=_params(("parallel",)),
        name="conv_in",
    )(x3.reshape(n, d), g.reshape(1, d), w_in.astype(BF16), b_in.reshape(1, 2 * d))
    u3 = u.reshape(b, s, d)
    halo_per_tile = tm // CONV_HALO
    row = lambda v: v.reshape(1, d)
    return pl.pallas_call(
        _conv_out_kernel,
        grid=(b, s // tm),
        in_specs=[pl.BlockSpec((1, tm, d), lambda bi, i: (bi, i, 0)),
                  pl.BlockSpec((1, tm, d), lambda bi, i: (bi, i, 0)),
                  pl.BlockSpec((1, CONV_HALO, d),
                               lambda bi, i: (bi, jnp.maximum(i * halo_per_tile - 1, 0), 0)),
                  _const_spec((CONV_WIDTH, SUBLANES, d)),
                  _const_spec((1, d)), _const_spec((1, d)), _const_spec((1, d)),
                  _const_spec((d, d)),
                  _const_spec((1, d))],
        out_specs=pl.BlockSpec((1, tm, d), lambda bi, i: (bi, i, 0)),
        out_shape=jax.ShapeDtypeStruct((b, s, d), F32),
        scratch_shapes=[pltpu.VMEM((SUBLANES, tm + CONV_HALO, d), F32), pltpu.VMEM((tm, d), BF16)],
        compiler_params=_params(("parallel", "parallel")),
        name="conv_out",
    )(x3, u3, u3, jnp.broadcast_to(w_dw[:, None, :], (CONV_WIDTH, SUBLANES, d)), row(b_dw),
      row(ln_g), row(ln_b), w_out.astype(BF16), row(b_out))


def _stage_logits(s, s_ref, mb_ref):
    s_ref[...] = s
    mb_ref[...] = jnp.max(s, axis=0, keepdims=True)


def _softmax_step_t(s_ref, mb_ref, vt, m_ref, l_ref, acc_ref):
    m_prev = m_ref[...]
    m_new = jnp.maximum(m_prev, mb_ref[...])
    alpha = jnp.exp2(m_prev - m_new)
    p = jnp.exp2(s_ref[...] - m_new)
    l_ref[...] = alpha * l_ref[...] + jnp.sum(p, axis=0, keepdims=True)
    acc_ref[...] = alpha * acc_ref[...] + jnp.dot(vt, p.astype(BF16), preferred_element_type=F32)
    m_ref[...] = m_new


FAR, PREV, DIAG = "far", "prev", "diag"


def _sweep_key_tiles(i, logits, absorb):
    def pipelined(tiles):
        logits(tiles[0][0], tiles[0][1], 0)
        for n, (j, _) in enumerate(tiles):
            if n + 1 < len(tiles):
                logits(tiles[n + 1][0], tiles[n + 1][1], (n + 1) % 2)
            absorb(j, n % 2)

    n_far = jnp.maximum(i - 1, 0)
    n_groups = n_far // FAR_UNROLL

    def far_group(jg, carry):
        pipelined([(jg * FAR_UNROLL + u, FAR) for u in range(FAR_UNROLL)])
        return carry

    lax.fori_loop(0, n_groups, far_group, 0)
    rest = n_groups * FAR_UNROLL

    for left in range(FAR_UNROLL):
        @pl.when(jnp.logical_and(i >= 1, n_far - rest == left))
        def _tail(left=left):
            pipelined([(rest + u, FAR) for u in range(left)] + [(i - 1, PREV), (i, DIAG)])

    @pl.when(i == 0)
    def _only_diag():
        pipelined([(i, DIAG)])


def _group_mean_matrix(groups, width):
    idx = []
    for gi, size in enumerate(groups):
        idx += [gi] * size
    reps = width // len(idx)
    gid = jnp.asarray([r * len(groups) + g for r in range(reps) for g in idx], jnp.int32)
    sizes = jnp.asarray([float(groups[g]) for _ in range(reps) for g in idx], F32)
    same = gid[:, None] == gid[None, :]
    return jnp.where(same, 1.0 / sizes[None, :], 0.0).astype(BF16)


def _diff_proj_kernel(x_ref, g_ref, wqt_ref, wk_ref, wvt_ref, gq_ref, gk_ref, gmat_ref,
                      qt_ref, k_ref, vt_ref):
    h = _rms_bf16(x_ref[0], g_ref[...])
    yqt = _dot_nt(wqt_ref[...], h)
    for r in range(yqt.shape[0] // DIFF_QK_DIM):
        sl = slice(r * DIFF_QK_DIM, (r + 1) * DIFF_QK_DIM)
        qt_ref[0, 0, sl, :] = _rows_rms(yqt[sl, :], gq_ref[sl, :]).astype(BF16)
    yk = jnp.dot(h, wk_ref[...], preferred_element_type=F32)
    width = gmat_ref.shape[0]
    for c in range(yk.shape[1] // width):
        sl = slice(c * width, (c + 1) * width)
        y = yk[:, sl]
        ms = _split_dot(y * y, gmat_ref[...])
        k_ref[0, :, sl] = (y * lax.rsqrt(ms + RMS_EPS) * gk_ref[:, sl]).astype(BF16)
    vt_ref[0, 0] = _dot_nt(wvt_ref[...], h).astype(BF16)


def _diff_attn_kernel(tab_ref, scal_ref, qt_ref, k_ref, vt_ref, bucket_ref, subg_ref, o_ref,
                      bias_ref, m_ref, l_ref, acc_ref, s_ref, mb_ref):
    t = qt_ref.shape[3]
    hd = pl.program_id(0)
    i = pl.program_id(2)

    @pl.when(jnp.logical_and(pl.program_id(1) == 0, i == 0))
    def _build_bias():
        far = tab_ref[REL_BUCKETS // 2 - 1, hd]
        keys = lax.broadcasted_iota(jnp.int32, (t, t), 0)
        queries = lax.broadcasted_iota(jnp.int32, (t, t), 1)
        visible = (keys // CHUNK) <= (queries // CHUNK)
        for tile in range(2):
            bk = bucket_ref[tile]
            bias = jnp.zeros((t, t), F32)
            for b in range(REL_BUCKETS):
                bias = jnp.where(bk == b, (tab_ref[b, hd] - far) * LOG2E, bias)
            if tile == 1:
                bias = jnp.where(visible, bias, MASK_VALUE)
            bias_ref[tile] = bias

    m_ref[...] = jnp.full(m_ref.shape, -jnp.inf, F32)
    l_ref[...] = jnp.zeros(l_ref.shape, F32)
    acc_ref[...] = jnp.zeros(acc_ref.shape, F32)

    qt = qt_ref[0, 0]
    feat = lax.broadcasted_iota(jnp.int32, qt.shape, 0)
    qmaps = [jnp.where((feat // DIFF_QK_DIM) == mp, qt, jnp.zeros_like(qt)) for mp in range(2)]

    bias_tile = {FAR: None, PREV: 0, DIAG: 1}

    def logits(j, kind, slot):
        k = k_ref[0, pl.ds(pl.multiple_of(j * t, t), t), :]
        ss = [jnp.dot(k, qmaps[mp], preferred_element_type=F32) for mp in range(2)]
        for mp in range(2):
            s = ss[mp] if kind == FAR else ss[mp] + bias_ref[bias_tile[kind]]
            _stage_logits(s, s_ref.at[slot, mp], mb_ref.at[slot, mp])

    def absorb(j, slot):
        vt = vt_ref[0, j]
        for mp in range(2):
            _softmax_step_t(s_ref.at[slot, mp], mb_ref.at[slot, mp], vt,
                            m_ref.at[mp], l_ref.at[mp], acc_ref.at[mp])

    _sweep_key_tiles(i, logits, absorb)

    lam = scal_ref[0]
    attn = acc_ref[0] / l_ref[0] - lam * (acc_ref[1] / l_ref[1])
    o_ref[0] = _rows_rms(attn, subg_ref[...]).T.astype(BF16)


def _t5_bucket(rel):
    nb = REL_BUCKETS // 2
    bucket = jnp.where(rel > 0, nb, 0)
    n = jnp.abs(rel)
    max_exact = nb // 2
    n_f = jnp.maximum(n, 1).astype(jnp.float32)
    large = max_exact + (jnp.log(n_f / max_exact) / math.log(REL_MAX_DIST / max_exact)
                         * (nb - max_exact)).astype(jnp.int32)
    large = jnp.minimum(large, nb - 1)
    return bucket + jnp.where(n < max_exact, n, large)


def _diff_mixer(x3, layer_idx, g, w_in, q_norm_g, k_norm_g, lam, sub_norm_g, w_out, rel_table):
    b, s, d = x3.shape
    t = TOK_TILE
    nb = s // t
    hh, dq, dv = DIFF_HEADS, DIFF_QK_DIM, DIFF_V_DIM
    w3 = w_in.reshape(d, hh, 4 * dq + dv)
    wqt = w3[:, :, :2 * dq].reshape(d, hh * 2 * dq).T.astype(BF16)
    wk = w3[:, :, 2 * dq:4 * dq].reshape(d, hh * 2 * dq).astype(BF16)
    wvt = w3[:, :, 4 * dq:].reshape(d, hh * dv).T.astype(BF16)
    scale = dq ** -0.5
    gq = (jnp.tile(q_norm_g, 2 * hh) * (scale * LOG2E)).reshape(hh * 2 * dq, 1)
    gk = jnp.tile(k_norm_g, 2 * hh).reshape(1, hh * 2 * dq)
    gmat = _group_mean_matrix([dq], 2 * LANES)
    feat_major = lambda w: pl.BlockSpec((1, 1, w, t), lambda bi, i: (bi, i, 0, 0))
    qt, k, vt = pl.pallas_call(
        _diff_proj_kernel,
        grid=(b, nb),
        in_specs=[pl.BlockSpec((1, t, d), lambda bi, i: (bi, i, 0)),
                  _const_spec((1, d)),
                  _const_spec((hh * 2 * dq, d)),
                  _const_spec((d, hh * 2 * dq)),
                  _const_spec((hh * dv, d)),
                  _const_spec((hh * 2 * dq, 1)),
                  _const_spec((1, hh * 2 * dq)),
                  _const_spec((2 * LANES, 2 * LANES))],
        out_specs=[feat_major(hh * 2 * dq),
                   pl.BlockSpec((1, t, hh * 2 * dq), lambda bi, i: (bi, i, 0)),
                   feat_major(hh * dv)],
        out_shape=[jax.ShapeDtypeStruct((b, nb, hh * 2 * dq, t), BF16),
                   jax.ShapeDtypeStruct((b, s, hh * 2 * dq), BF16),
                   jax.ShapeDtypeStruct((b, nb, hh * dv, t), BF16)],
        compiler_params=_params(("parallel", "parallel")),
        name="diff_proj",
    )(x3, g.reshape(1, d), wqt, wk, wvt, gq, gk, gmat)

    lam_init = LAMBDA_INIT_BASE - LAMBDA_INIT_SCALE * math.exp(-LAMBDA_INIT_DECAY * layer_idx)
    lam_full = (jnp.exp(jnp.sum(lam[0] * lam[1]).astype(F32))
                - jnp.exp(jnp.sum(lam[2] * lam[3]).astype(F32)) + lam_init)
    scal = jnp.reshape(lam_full, (1,)).astype(F32)
    r = jnp.arange(t)
    rel_diag = r[:, None] - r[None, :]
    buckets = jnp.stack([_t5_bucket(rel_diag - t), _t5_bucket(rel_diag)]).astype(jnp.int32)
    subg = (sub_norm_g * (1.0 - lam_init)).reshape(dv, 1)
    smem = pl.BlockSpec(memory_space=pltpu.SMEM)
    o = pl.pallas_call(
        _diff_attn_kernel,
        grid=(hh, b, nb),
        in_specs=[smem, smem,
                  pl.BlockSpec((1, 1, 2 * dq, t), lambda h, bi, i: (bi, i, h, 0)),
                  pl.BlockSpec((1, s, 2 * dq), lambda h, bi, i: (bi, 0, h)),
                  pl.BlockSpec((1, nb, dv, t), lambda h, bi, i: (bi, 0, h, 0)),
                  _const_spec((2, t, t)),
                  _const_spec((dv, 1))],
        out_specs=pl.BlockSpec((1, t, dv), lambda h, bi, i: (bi, i, h)),
        out_shape=jax.ShapeDtypeStruct((b, s, hh * dv), BF16),
        scratch_shapes=[pltpu.VMEM((2, t, t), F32),
                        pltpu.VMEM((2, 1, t), F32),
                        pltpu.VMEM((2, 1, t), F32),
                        pltpu.VMEM((2, dv, t), F32),
                        pltpu.VMEM((2, 2, t, t), F32),
                        pltpu.VMEM((2, 2, 1, t), F32)],
        compiler_params=_params(("parallel", "arbitrary", "arbitrary")),
        name="diff_attn",
    )(rel_table.astype(F32), scal, qt, k, vt, buckets, subg)
    return o.reshape(b * s, hh * dv), w_out


def _sb_proj_kernel(x_ref, g_ref, wqt_ref, wk_ref, wvt_ref, qt_ref, k_ref, vt_ref):
    h = _rms_bf16(x_ref[0], g_ref[...])
    qt_ref[0, 0] = (_dot_nt(wqt_ref[...], h) * (SB_HEAD_DIM ** -0.5 * LOG2E)).astype(BF16)
    k_ref[0] = jnp.dot(h, wk_ref[...], preferred_element_type=F32).astype(BF16)
    yvt = _dot_nt(wvt_ref[...], h).astype(BF16)
    sk = vt_ref.shape[3]
    for c in range(vt_ref.shape[1]):
        vt_ref[0, c] = yvt[:, c * sk:(c + 1) * sk]


def _sb_attn_kernel(qt_ref, k_ref, vt_ref, tri_ref, o_ref, run_ref, acc_ref, base_ref, tot_ref,
                    top_ref):
    t = qt_ref.shape[3]
    kb = SB_KEY_BLOCK
    sk = SB_SWEEP_KEYS
    dh = SB_HEAD_DIM
    i = pl.program_id(2)
    sub = sk // kb
    qt = qt_ref[0, 0]
    feat = lax.broadcasted_iota(jnp.int32, qt.shape, 0)
    qh = [jnp.where((feat // dh) == a, qt, jnp.zeros_like(qt)) for a in range(2)]
    run_ref[...] = jnp.zeros(run_ref.shape, F32)
    acc_ref[...] = jnp.zeros(acc_ref.shape, F32)

    def logits(j, c0):
        k = k_ref[0, pl.ds(pl.multiple_of(j * sk, sk), sk), :]
        return [jnp.dot(k, qh[a][:, c0:], preferred_element_type=F32) for a in range(2)]

    def causal_mask(c0):
        shape = (sk, t - c0)
        return lax.broadcasted_iota(jnp.int32, shape, 0) < lax.broadcasted_iota(jnp.int32, shape, 1)

    def prepare(zs, diagonal, c0, slot):
        qs = slice(c0, t)
        for a in range(2):
            z = zs[a]
            soft = jnp.log2(1.0 + jnp.exp2(-jnp.abs(z)))
            log_beta = jnp.minimum(z, 0.0) - soft
            log_keep = log_beta - z
            if diagonal:
                log_keep = jnp.where(causal_mask(c0), log_keep, 0.0)
            hi = log_keep.astype(BF16)
            lo = (log_keep - hi.astype(F32)).astype(BF16)
            for blk in range(sub):
                rows = slice(blk * kb, (blk + 1) * kb)
                sums = jnp.dot(tri_ref[...], jnp.concatenate([hi[rows], lo[rows]], axis=0),
                               preferred_element_type=F32)
                base_ref[slot, a, rows, qs] = log_beta[rows] + sums[:kb]
                tot_ref[slot, a, blk, :, qs] = sums[kb:]

    def fold(j, diagonal, c0, slot, publish_top=False):
        qs = slice(c0, t)
        vt = vt_ref[0, j]
        weights = []
        for a in range(2):
            run = run_ref[a, :, qs]
            ws = [None] * sub
            for blk in range(sub - 1, -1, -1):
                rows = slice(blk * kb, (blk + 1) * kb)
                ws[blk] = jnp.exp2(base_ref[slot, a, rows, qs] + run[0:1])
                run = run + tot_ref[slot, a, blk, :, qs]
            run_ref[a, :, qs] = run
            weights.append(jnp.concatenate(ws, axis=0))
        if publish_top:
            top_ref[0] = jnp.max(run_ref[...])
        for a in range(2):
            w = weights[a]
            if diagonal:
                w = jnp.where(causal_mask(c0), w, 0.0)
            acc_ref[a, :, qs] += jnp.dot(vt[a * dh:(a + 1) * dh, :], w.astype(BF16),
                                         preferred_element_type=F32)

    per_tile = t // sk
    own = [(i * per_tile + b, True, b * sk) for b in range(per_tile - 1, -1, -1)]

    def run_blocks(blocks):
        zs = logits(blocks[0][0], blocks[0][2])
        for n, (j, diagonal, c0) in enumerate(blocks):
            nxt = logits(blocks[n + 1][0], blocks[n + 1][2]) if n + 1 < len(blocks) else None
            prepare(zs, diagonal, c0, n)
            if n >= 1:
                fold(blocks[n - 1][0], blocks[n - 1][1], blocks[n - 1][2], n - 1)
            zs = nxt
        last = len(blocks) - 1
        fold(blocks[last][0], blocks[last][1], blocks[last][2], last, publish_top=True)

    @pl.when(i >= 1)
    def _with_previous():
        run_blocks(own + [(i * per_tile - 1, False, 0)])

    @pl.when(i == 0)
    def _first_tile():
        run_blocks(own)

    def cond(c):
        j, top = c
        return jnp.logical_and(j >= 0, top > F32_EXP2_ZERO)

    def body(c):
        j, _ = c
        run_blocks([(j, False, 0)])
        return j - 1, top_ref[0]

    lax.while_loop(cond, body, (i * per_tile - 2, top_ref[0]))
    o_ref[0] = jnp.concatenate([acc_ref[0], acc_ref[1]], axis=0).T.astype(BF16)


def _sb_mixer(x3, g, w_in, w_out):
    b, s, d = x3.shape
    t = TOK_TILE
    kb = SB_KEY_BLOCK
    sweep = SB_SWEEP_KEYS
    nb = s // t
    hh, dh = SB_HEADS, SB_HEAD_DIM
    w3 = w_in.reshape(d, hh, 3 * dh)
    wqt = w3[:, :, :dh].reshape(d, hh * dh).T.astype(BF16)
    wk = w3[:, :, dh:2 * dh].reshape(d, hh * dh).astype(BF16)
    wvt = w3[:, :, 2 * dh:].reshape(d, hh * dh).T.astype(BF16)
    qt, k, vt = pl.pallas_call(
        _sb_proj_kernel,
        grid=(b, nb),
        in_specs=[pl.BlockSpec((1, t, d), lambda bi, i: (bi, i, 0)),
                  _const_spec((1, d)),
                  _const_spec((hh * dh, d)),
                  _const_spec((d, hh * dh)),
                  _const_spec((hh * dh, d))],
        out_specs=[pl.BlockSpec((1, 1, hh * dh, t), lambda bi, i: (bi, i, 0, 0)),
                   pl.BlockSpec((1, t, hh * dh), lambda bi, i: (bi, i, 0)),
                   pl.BlockSpec((1, t // sweep, hh * dh, sweep), lambda bi, i: (bi, i, 0, 0))],
        out_shape=[jax.ShapeDtypeStruct((b, nb, hh * dh, t), BF16),
                   jax.ShapeDtypeStruct((b, s, hh * dh), BF16),
                   jax.ShapeDtypeStruct((b, s // sweep, hh * dh, sweep), BF16)],
        compiler_params=_params(("parallel", "parallel")),
        name="sb_proj",
    )(x3, g.reshape(1, d), wqt, wk, wvt)

    rj = jnp.arange(kb + SUBLANES)
    sk = jnp.arange(2 * kb) % kb
    tri = jnp.where((rj[:, None] >= kb) | (sk[None, :] > rj[:, None]), 1.0, 0.0).astype(BF16)
    pair = 2 * dh
    o = pl.pallas_call(
        _sb_attn_kernel,
        grid=(b, hh // 2, nb),
        in_specs=[pl.BlockSpec((1, 1, pair, t), lambda bi, h, i: (bi, i, h, 0)),
                  pl.BlockSpec((1, s, pair), lambda bi, h, i: (bi, 0, h)),
                  pl.BlockSpec((1, s // sweep, pair, sweep), lambda bi, h, i: (bi, 0, h, 0)),
                  _const_spec((kb + SUBLANES, 2 * kb))],
        out_specs=pl.BlockSpec((1, t, pair), lambda bi, h, i: (bi, i, h)),
        out_shape=jax.ShapeDtypeStruct((b, s, hh * dh), BF16),
        scratch_shapes=[pltpu.VMEM((2, SUBLANES, t), F32), pltpu.VMEM((2, dh, t), F32),
                        pltpu.VMEM((t // sweep + 1, 2, sweep, t), F32),
                        pltpu.VMEM((t // sweep + 1, 2, sweep // kb, SUBLANES, t), F32),
                        pltpu.SMEM((1,), F32)],
        compiler_params=_params(("parallel", "parallel", "arbitrary")),
        name="sb_attn",
    )(qt, k, vt, tri)
    return o.reshape(b * s, hh * dh), w_out


MLA_QK_PAD = MLA_NOPE + 2 * MLA_ROPE


def _mla_proj_kernel(x_ref, g_ref, waq_ref, wakv_ref, war_ref, gqa_ref, gkva_ref, wuqt_ref,
                     wuk_ref, wuvt_ref, gmat_ref, gq_ref, cst_ref, gkn_ref, gkr_ref, csk_ref,
                     qt_ref, k_ref, vt_ref):
    nope, rope, qp = MLA_NOPE, MLA_ROPE, MLA_QK_PAD
    h = _rms_bf16(x_ref[0], g_ref[...])
    cq = _rms_bf16(jnp.dot(h, waq_ref[...], preferred_element_type=F32), gqa_ref[...])
    ckv = _rms_bf16(jnp.dot(h, wakv_ref[...], preferred_element_type=F32), gkva_ref[...])
    yqt = _dot_nt(wuqt_ref[...], cq)
    cs = cst_ref[...]
    for hd in range(MLA_HEADS):
        base = hd * qp
        y = yqt[base:base + qp, :]
        gcol = gq_ref[base:base + qp, :]
        qt_ref[0, 0, base:base + nope, :] = _rows_rms(y[:nope], gcol[:nope]).astype(BF16)
        msr = jnp.mean(y[nope:nope + rope] * y[nope:nope + rope], axis=0, keepdims=True)
        rr = y[nope:] * lax.rsqrt(msr + RMS_EPS) * gcol[nope:] * cs
        qf = (rr[:rope] + rr[rope:]).astype(BF16)
        qt_ref[0, 0, base + nope:base + nope + rope, :] = qf
        qt_ref[0, 0, base + nope + rope:base + qp, :] = qf
    width = gmat_ref.shape[0]
    gm = gmat_ref[...]
    sh = jnp.dot(h, war_ref[...], preferred_element_type=F32)
    ms = _split_dot(sh * sh, gm[:qp, :qp])
    shn = sh * lax.rsqrt(ms + RMS_EPS) * gkr_ref[...] * csk_ref[...]
    shared = jnp.concatenate([shn] * (width // qp), axis=1)
    ykn = jnp.dot(ckv, wuk_ref[...], preferred_element_type=F32)
    for c in range(ykn.shape[1] // width):
        sl = slice(c * width, (c + 1) * width)
        y = ykn[:, sl]
        ms = _split_dot(y * y, gm)
        k_ref[0, :, sl] = (y * lax.rsqrt(ms + RMS_EPS) * gkn_ref[:, sl] + shared).astype(BF16)
    vt_ref[0, 0] = _dot_nt(wuvt_ref[...], ckv).astype(BF16)


def _mla_attn_kernel(qt_ref, k_ref, vt_ref, o_ref, m_ref, l_ref, acc_ref, s_ref, mb_ref):
    t = qt_ref.shape[3]
    qp, dv = MLA_QK_PAD, MLA_V
    i = pl.program_id(2)
    m_ref[...] = jnp.full(m_ref.shape, -jnp.inf, F32)
    l_ref[...] = jnp.zeros(l_ref.shape, F32)
    acc_ref[...] = jnp.zeros(acc_ref.shape, F32)

    def logits(j, kind, slot):
        k = k_ref[0, pl.ds(pl.multiple_of(j * t, t), t), :]
        ss = [jnp.dot(k[:, a * qp:(a + 1) * qp], qt_ref[0, 0, a * qp:(a + 1) * qp, :],
                      preferred_element_type=F32) for a in range(2)]
        for a in range(2):
            s = ss[a]
            if kind == DIAG:
                keys = lax.broadcasted_iota(jnp.int32, (t, t), 0)
                queries = lax.broadcasted_iota(jnp.int32, (t, t), 1)
                s = jnp.where((keys // CHUNK) <= (queries // CHUNK), s, MASK_VALUE)
            _stage_logits(s, s_ref.at[slot, a], mb_ref.at[slot, a])

    def absorb(j, slot):
        vt = vt_ref[0, j]
        for a in range(2):
            _softmax_step_t(s_ref.at[slot, a], mb_ref.at[slot, a], vt[a * dv:(a + 1) * dv, :],
                            m_ref.at[a], l_ref.at[a], acc_ref.at[a])

    _sweep_key_tiles(i, logits, absorb)
    o = jnp.concatenate([acc_ref[0] / l_ref[0], acc_ref[1] / l_ref[1]], axis=0)
    o_ref[0] = o.T.astype(BF16)


def _rotate_half_cols(w):
    half = MLA_ROPE // 2
    return jnp.concatenate([-w[..., half:], w[..., :half]], axis=-1)


def _mla_mixer(x3, g, w_a, q_a_norm, kv_a_norm, w_uq, w_ukv, q_norm_g, k_norm_g, w_out):
    b, s, d = x3.shape
    t = TOK_TILE
    nb = s // t
    hh = MLA_HEADS
    nope, rope, qp, dv = MLA_NOPE, MLA_ROPE, MLA_QK_PAD, MLA_V
    half = rope // 2
    wa_q = w_a[:, :MLA_Q_LORA].astype(BF16)
    wa_kv = w_a[:, MLA_Q_LORA:MLA_Q_LORA + MLA_KV_LORA].astype(BF16)
    wa_r = w_a[:, MLA_Q_LORA + MLA_KV_LORA:]
    war = jnp.concatenate([jnp.zeros((d, nope), F32), wa_r, _rotate_half_cols(wa_r)], axis=1).astype(BF16)
    wq3 = w_uq.reshape(MLA_Q_LORA, hh, nope + rope)
    wuqt = jnp.concatenate([wq3, _rotate_half_cols(wq3[:, :, nope:])], axis=2
                           ).reshape(MLA_Q_LORA, hh * qp).T.astype(BF16)
    wkv3 = w_ukv.reshape(MLA_KV_LORA, hh, nope + dv)
    wuk = jnp.concatenate([wkv3[:, :, :nope], jnp.zeros((MLA_KV_LORA, hh, 2 * rope), F32)], axis=2
                          ).reshape(MLA_KV_LORA, hh * qp).astype(BF16)
    wuvt = wkv3[:, :, nope:].reshape(MLA_KV_LORA, hh * dv).T.astype(BF16)
    swap = lambda v: jnp.concatenate([v[half:], v[:half]])
    scale = (nope + rope) ** -0.5
    gq_head = jnp.concatenate([q_norm_g, swap(q_norm_g[nope:])]) * (scale * LOG2E)
    gq = jnp.tile(gq_head, hh).reshape(hh * qp, 1)
    gkn = jnp.tile(jnp.concatenate([k_norm_g[:nope], jnp.zeros((2 * rope,), F32)]), hh).reshape(1, hh * qp)
    gkr = jnp.concatenate([jnp.zeros((nope,), F32), k_norm_g[nope:], swap(k_norm_g[nope:])]).reshape(1, qp)
    inv = ROPE_BASE ** (-jnp.arange(0, rope, 2, dtype=F32) / rope)
    ang = jnp.arange(s, dtype=F32)[:, None] * inv[None, :]
    cos, sin = jnp.cos(ang), jnp.sin(ang)
    cs = jnp.concatenate([cos, cos, sin, sin], axis=1)
    csk = jnp.concatenate([jnp.zeros((s, nope), F32), cs], axis=1)
    gmat = _group_mean_matrix([nope, rope, rope], 2 * qp)
    feat_major = lambda w: pl.BlockSpec((1, 1, w, t), lambda bi, i: (bi, i, 0, 0))
    qt, k, vt = pl.pallas_call(
        _mla_proj_kernel,
        grid=(b, nb),
        in_specs=[pl.BlockSpec((1, t, d), lambda bi, i: (bi, i, 0)),
                  _const_spec((1, d)),
                  _const_spec((d, MLA_Q_LORA)),
                  _const_spec((d, MLA_KV_LORA)),
                  _const_spec((d, qp)),
                  _const_spec((1, MLA_Q_LORA)),
                  _const_spec((1, MLA_KV_LORA)),
                  _const_spec((hh * qp, MLA_Q_LORA)),
                  _const_spec((MLA_KV_LORA, hh * qp)),
                  _const_spec((hh * dv, MLA_KV_LORA)),
                  _const_spec((2 * qp, 2 * qp)),
                  _const_spec((hh * qp, 1)),
                  pl.BlockSpec((2 * rope, t), lambda bi, i: (0, i)),
                  _const_spec((1, hh * qp)),
                  _const_spec((1, qp)),
                  pl.BlockSpec((t, qp), lambda bi, i: (i, 0))],
        out_specs=[feat_major(hh * qp),
                   pl.BlockSpec((1, t, hh * qp), lambda bi, i: (bi, i, 0)),
                   feat_major(hh * dv)],
        out_shape=[jax.ShapeDtypeStruct((b, nb, hh * qp, t), BF16),
                   jax.ShapeDtypeStruct((b, s, hh * qp), BF16),
                   jax.ShapeDtypeStruct((b, nb, hh * dv, t), BF16)],
        compiler_params=_params(("parallel", "parallel")),
        name="mla_proj",
    )(x3, g.reshape(1, d), wa_q, wa_kv, war, q_a_norm.reshape(1, -1), kv_a_norm.reshape(1, -1),
      wuqt, wuk, wuvt, gmat, gq, cs.T, gkn, gkr, csk)

    o = pl.pallas_call(
        _mla_attn_kernel,
        grid=(b, hh // 2, nb),
        in_specs=[pl.BlockSpec((1, 1, 2 * qp, t), lambda bi, h, i: (bi, i, h, 0)),
                  pl.BlockSpec((1, s, 2 * qp), lambda bi, h, i: (bi, 0, h)),
                  pl.BlockSpec((1, nb, 2 * dv, t), lambda bi, h, i: (bi, 0, h, 0))],
        out_specs=pl.BlockSpec((1, t, 2 * dv), lambda bi, h, i: (bi, i, h)),
        out_shape=jax.ShapeDtypeStruct((b, s, hh * dv), BF16),
        scratch_shapes=[pltpu.VMEM((2, 1, t), F32), pltpu.VMEM((2, 1, t), F32),
                        pltpu.VMEM((2, dv, t), F32),
                        pltpu.VMEM((2, 2, t, t), F32),
                        pltpu.VMEM((2, 2, 1, t), F32)],
        compiler_params=_params(("parallel", "parallel", "arbitrary")),
        name="mla_attn",
    )(qt, k, vt)
    return o.reshape(b * s, hh * dv), w_out


def kernel(x, rel_bias_table, ffn_norm, ffn_w_in, ffn_w_out, mixer_norm, conv_w_in, conv_b_in, conv_w_dw, conv_b_dw, conv_ln_g, conv_ln_b, conv_w_out, conv_b_out, diff_w_in, diff_q_norm, diff_k_norm, diff_lambda, diff_sub_norm, diff_w_out, sb_w_in, sb_w_out, mla_w_a, mla_q_a_norm, mla_kv_a_norm, mla_w_uq, mla_w_ukv, mla_q_norm, mla_k_norm, mla_w_out):
    b, s, d = x.shape
    assert d == D_MODEL and s % TOK_TILE == 0
    depth = ffn_norm.shape[0]
    ffn = lambda xx, i, k, proj=None: _ffn(xx.reshape(b * s, d), ffn_norm[i, k], ffn_w_in, ffn_w_out,
                                           i, k, proj).reshape(b, s, d)
    for i in range(depth):
        mixer, j = i % N_MIXERS, i // N_MIXERS
        x = ffn(x, i, 0)
        g = mixer_norm[i]
        proj = None
        if mixer == 0:
            x = _conv_mixer(x, g, conv_w_in[j], conv_b_in[j], conv_w_dw[j], conv_b_dw[j],
                            conv_ln_g[j], conv_ln_b[j], conv_w_out[j], conv_b_out[j])
        elif mixer == 1:
            proj = _diff_mixer(x, i, g, diff_w_in[j], diff_q_norm[j], diff_k_norm[j], diff_lambda[j],
                               diff_sub_norm[j], diff_w_out[j], rel_bias_table)
        elif mixer == 2:
            proj = _sb_mixer(x, g, sb_w_in[j], sb_w_out[j])
        else:
            proj = _mla_mixer(x, g, mla_w_a[j], mla_q_a_norm[j], mla_kv_a_norm[j], mla_w_uq[j],
                              mla_w_ukv[j], mla_q_norm[j], mla_k_norm[j], mla_w_out[j])
        x = ffn(x, i, 1, proj)
    return x
```
